```python
import math
import jax, jax.numpy as jnp
from jax import lax
import numpy as np

D_MODEL = 2048
BATCH = 8
SEQ = 2048
DEPTH = 2

GDN_HEAD_DIM = 128
GDN_HEADS = D_MODEL // 256
GDN_WIDTH = GDN_HEADS * GDN_HEAD_DIM
GDN_CONV = 4
GDN_CHUNK = 64
SB_HEAD_DIM = 128
SB_HEADS = D_MODEL // 256
SB_WIDTH = SB_HEADS * SB_HEAD_DIM
SB_BLOCK = 128
MIX_WIDTH = GDN_WIDTH + SB_WIDTH
IN_SIZES = (GDN_WIDTH, GDN_WIDTH, GDN_WIDTH, GDN_WIDTH, GDN_HEADS, GDN_HEADS,
            SB_WIDTH, SB_WIDTH, SB_WIDTH)
IN_COLS = sum(IN_SIZES)
IN_OFFSETS = tuple(int(o) for o in np.cumsum(IN_SIZES)[:-1])
D_FF = ((8 * D_MODEL // 3 + 255) // 256) * 256
FFN_CONV = 3
RMS_EPS = 1e-6
L2_EPS = 1e-6

kernel_name = "hybrid_gdn_stickbreaking_convffn"


def rms_norm(x, gain, eps=RMS_EPS):
    xf = x.astype(jnp.float32)
    y = xf * lax.rsqrt(jnp.mean(xf * xf, axis=-1, keepdims=True) + eps)
    return (y * gain.astype(jnp.float32)).astype(x.dtype)


def l2_norm(x):
    return x * lax.rsqrt(jnp.sum(x * x, axis=-1, keepdims=True) + L2_EPS)


def causal_dwconv(x, w):
    width, channels = w.shape
    return lax.conv_general_dilated(
        x, w[:, None, :].astype(x.dtype), window_strides=(1,), padding=[(width - 1, 0)],
        dimension_numbers=("NWC", "WIO", "NWC"), feature_group_count=channels)


def chunk_gated_delta_rule(q, k, v, g, beta):
    B, S, H, Dk = q.shape
    Dv = v.shape[-1]
    C = GDN_CHUNK
    N = S // C

    def to_chunks(t):
        return t.reshape(B, N, C, H, t.shape[-1]).transpose(1, 0, 3, 2, 4)

    q, k, v = to_chunks(q), to_chunks(k), to_chunks(v)
    g = g.reshape(B, N, C, H).transpose(1, 0, 3, 2)
    beta = beta.reshape(B, N, C, H).transpose(1, 0, 3, 2)
    g = jnp.cumsum(g, axis=-1)
    idx = jnp.arange(C)
    tril = idx[:, None] >= idx[None, :]
    strict = idx[:, None] > idx[None, :]
    decay = jnp.exp(jnp.where(tril, g[..., :, None] - g[..., None, :], -jnp.inf))
    k_beta = k * beta[..., None]
    v_beta = v * beta[..., None]
    lower = jnp.where(strict, jnp.einsum('nbhcd,nbhed->nbhce', k_beta, k) * decay, 0.0)
    eye = jnp.eye(C, dtype=jnp.float32)
    t_inv = lax.linalg.triangular_solve(eye + lower, jnp.broadcast_to(eye, lower.shape),
                                        left_side=True, lower=True, unit_diagonal=True)
    u = jnp.einsum('nbhce,nbhed->nbhcd', t_inv, v_beta)
    w = jnp.einsum('nbhce,nbhed->nbhcd', t_inv, k_beta * jnp.exp(g)[..., None])
    intra = jnp.where(tril, jnp.einsum('nbhcd,nbhed->nbhce', q, k) * decay, 0.0)

    def step(state, inp):
        q_i, k_i, u_i, w_i, g_i, a_i = inp
        v_new = u_i - jnp.einsum('bhcd,bhde->bhce', w_i, state)
        o_i = (jnp.einsum('bhcd,bhde->bhce', q_i * jnp.exp(g_i)[..., None], state)
               + jnp.einsum('bhce,bhed->bhcd', a_i, v_new))
        g_last = g_i[..., -1]
        k_dec = k_i * jnp.exp(g_last[..., None] - g_i)[..., None]
        state = state * jnp.exp(g_last)[..., None, None] + jnp.einsum('bhcd,bhce->bhde', k_dec, v_new)
        return state, o_i

    state0 = jnp.zeros((B, H, Dk, Dv), jnp.float32)
    _, o = lax.scan(step, state0, (q, k, u, w, g, intra))
    return o.transpose(1, 0, 3, 2, 4).reshape(B, S, H, Dv)


def gated_deltanet(q, k, v, z, b, a, conv_w, a_log, dt_bias, o_gain):
    B, S, _ = q.shape
    qkv = jax.nn.silu(causal_dwconv(jnp.concatenate([q, k, v], axis=-1), conv_w))
    qkv = qkv.astype(jnp.float32).reshape(B, S, 3, GDN_HEADS, GDN_HEAD_DIM)
    q = l2_norm(qkv[:, :, 0]) * (GDN_HEAD_DIM ** -0.5)
    k = l2_norm(qkv[:, :, 1])
    v = qkv[:, :, 2]
    beta = jax.nn.sigmoid(b.astype(jnp.float32))
    g = -jnp.exp(a_log.astype(jnp.float32)) * jax.nn.softplus(
        a.astype(jnp.float32) + dt_bias.astype(jnp.float32))
    o = chunk_gated_delta_rule(q, k, v, g, beta)
    gate = jax.nn.silu(z.astype(jnp.float32).reshape(B, S, GDN_HEADS, GDN_HEAD_DIM))
    o = rms_norm(o, o_gain) * gate
    return o.reshape(B, S, GDN_WIDTH)


def stick_breaking_attention(q, k, v):
    B, H, S, D = q.shape
    scale = D ** -0.5
    outs = []
    for blk in range(S // SB_BLOCK):
        start = blk * SB_BLOCK
        end = start + SB_BLOCK
        k_p, v_p = k[:, :, :end], v[:, :, :end]
        z = jnp.einsum('bhtd,bhsd->bhts', q[:, :, start:end], k_p) * scale
        past = jnp.arange(end)[None, :] < (start + jnp.arange(SB_BLOCK))[:, None]
        log_beta = jax.nn.log_sigmoid(z)
        log_keep = jnp.where(past, log_beta - z, 0.0)
        between = lax.cumsum(log_keep, axis=3, reverse=True) - log_keep
        weights = jnp.where(past, jnp.exp(log_beta + between), 0.0)
        outs.append(jnp.einsum('bhts,bhsd->bhtd', weights, v_p))
    return jnp.concatenate(outs, axis=2)


def setup_inputs(seed: int = 0) -> dict:
    key = jax.random.key(seed)
    ks = jax.random.split(key, 16)
    f32 = jnp.float32

    def gain(k, n):
        return 1.0 + 0.02 * jax.random.normal(k, (DEPTH, n), f32)

    x = jax.random.normal(ks[0], (BATCH, SEQ, D_MODEL), f32)
    attn_norm = gain(ks[1], D_MODEL)
    w_in = jax.random.normal(ks[2], (DEPTH, D_MODEL, IN_COLS), f32) * D_MODEL ** -0.5
    gdn_conv = jax.random.normal(ks[3], (DEPTH, GDN_CONV, 3 * GDN_WIDTH), f32) * GDN_CONV ** -0.5
    gdn_a_log = jnp.log(jax.random.uniform(ks[4], (DEPTH, GDN_HEADS), f32, 1.0, 16.0))
    dt = jnp.exp(jax.random.uniform(ks[5], (DEPTH, GDN_HEADS), f32, math.log(1e-3), math.log(1e-1)))
    gdn_dt_bias = dt + jnp.log(-jnp.expm1(-dt))
    gdn_o_norm = gain(ks[6], GDN_HEAD_DIM)
    sb_q_norm = gain(ks[7], SB_HEAD_DIM)
    sb_k_norm = gain(ks[8], SB_HEAD_DIM)
    sb_o_norm = gain(ks[9], SB_HEAD_DIM)
    w_out = jax.random.normal(ks[10], (DEPTH, MIX_WIDTH, D_MODEL), f32) * MIX_WIDTH ** -0.5
    ffn_norm = gain(ks[11], D_MODEL)
    w_up = jax.random.normal(ks[12], (DEPTH, D_MODEL, 2 * D_FF), f32) * D_MODEL ** -0.5
    ffn_conv = jax.random.normal(ks[13], (DEPTH, FFN_CONV, 2 * D_FF), f32) * FFN_CONV ** -0.5
    ffn_conv_bias = 0.02 * jax.random.normal(ks[14], (DEPTH, 2 * D_FF), f32)
    w_down = jax.random.normal(ks[15], (DEPTH, D_FF, D_MODEL), f32) * D_FF ** -0.5
    return {"x": x, "attn_norm": attn_norm, "w_in": w_in, "gdn_conv": gdn_conv,
            "gdn_a_log": gdn_a_log, "gdn_dt_bias": gdn_dt_bias, "gdn_o_norm": gdn_o_norm,
            "sb_q_norm": sb_q_norm, "sb_k_norm": sb_k_norm, "sb_o_norm": sb_o_norm,
            "w_out": w_out, "ffn_norm": ffn_norm, "w_up": w_up, "ffn_conv": ffn_conv,
            "ffn_conv_bias": ffn_conv_bias, "w_down": w_down}


def reference(x, attn_norm, w_in, gdn_conv, gdn_a_log, gdn_dt_bias, gdn_o_norm,
              sb_q_norm, sb_k_norm, sb_o_norm, w_out, ffn_norm, w_up, ffn_conv,
              ffn_conv_bias, w_down):
    B, S, _ = x.shape
    for l in range(DEPTH):
        h = rms_norm(x, attn_norm[l])
        proj = h @ w_in[l]
        gq, gk, gv, gz, gb, ga, sq, sk, sv = jnp.split(proj, IN_OFFSETS, axis=-1)
        y_gdn = gated_deltanet(gq, gk, gv, gz, gb, ga, gdn_conv[l], gdn_a_log[l],
                               gdn_dt_bias[l], gdn_o_norm[l])
        sq = rms_norm(sq.astype(jnp.float32).reshape(B, S, SB_HEADS, SB_HEAD_DIM), sb_q_norm[l])
        sk = rms_norm(sk.astype(jnp.float32).reshape(B, S, SB_HEADS, SB_HEAD_DIM), sb_k_norm[l])
        sv = sv.astype(jnp.float32).reshape(B, S, SB_HEADS, SB_HEAD_DIM)
        y_sb = stick_breaking_attention(sq.transpose(0, 2, 1, 3), sk.transpose(0, 2, 1, 3),
                                        sv.transpose(0, 2, 1, 3))
        y_sb = rms_norm(y_sb.transpose(0, 2, 1, 3), sb_o_norm[l]).reshape(B, S, SB_WIDTH)
        mixed = jnp.concatenate([y_gdn, y_sb], axis=-1).astype(x.dtype)
        x = x + mixed @ w_out[l]
        h = rms_norm(x, ffn_norm[l])
        up = causal_dwconv(h @ w_up[l], ffn_conv[l]) + ffn_conv_bias[l]
        gate, val = jnp.split(up, 2, axis=-1)
        x = x + (jax.nn.silu(gate) * val) @ w_down[l]
    return x
```

```python
import functools

import jax
import jax.numpy as jnp
from jax import lax
from jax.experimental import pallas as pl
from jax.experimental.pallas import tpu as pltpu

F32 = jnp.float32
BF16 = jnp.bfloat16
RMS_EPS = 1e-6
L2_EPS = 1e-6
HEAD_DIM = 128
LANES = 128
GDN_CHUNK = 64
GDN_CONV = 4
FFN_CONV = 3
SB_BLOCK = 128
HALO = 16
NEG_BIG = -1e30
VMEM_LIMIT = 56 * 1024 * 1024


def _dot(a, b):
    return jnp.dot(a, b, preferred_element_type=F32)


def _dot_nt(a, b):
    return lax.dot_general(a, b, (((1,), (1,)), ((), ())), preferred_element_type=F32)


def _dot_tn(a, b):
    return lax.dot_general(a, b, (((0,), (0,)), ((), ())), preferred_element_type=F32)


def _sigmoid(x):
    return 1.0 / (1.0 + jnp.exp(-x))


def _softplus(x):
    return jnp.maximum(x, 0.0) + jnp.log1p(jnp.exp(-jnp.abs(x)))


def _rms(x, gain):
    return (x * lax.rsqrt(jnp.mean(x * x, axis=-1, keepdims=True) + RMS_EPS)) * gain


def _params(n_axes):
    return pltpu.CompilerParams(dimension_semantics=("arbitrary",) * n_axes, vmem_limit_bytes=VMEM_LIMIT)


def _inproj_kernel(x_ref, g_ref, w_ref, wba_ref, o_ref, ba_ref, h_scr, *, n_slabs):
    @pl.when(pl.program_id(2) == 0)
    def _():
        hb = _rms(x_ref[0], g_ref[...]).astype(BF16)
        h_scr[...] = hb
        ba_ref[0] = _dot(hb, wba_ref[...])

    res = _dot(h_scr[...], w_ref[...])
    for c in range(n_slabs):
        o_ref[0, c] = res[:, c * LANES:(c + 1) * LANES]


def _inproj(x, gain, w_main, w_ba, *, tm, tn):
    B, S, D = x.shape
    N = w_main.shape[1]
    n_slabs = tn // LANES
    return pl.pallas_call(
        functools.partial(_inproj_kernel, n_slabs=n_slabs),
        grid=(B, S // tm, N // tn),
        in_specs=[
            pl.BlockSpec((1, tm, D), lambda b, i, j: (b, i, 0)),
            pl.BlockSpec((1, D), lambda b, i, j: (0, 0)),
            pl.BlockSpec((D, tn), lambda b, i, j: (0, j)),
            pl.BlockSpec((D, LANES), lambda b, i, j: (0, 0)),
        ],
        out_specs=[
            pl.BlockSpec((1, n_slabs, tm, LANES), lambda b, i, j: (b, j, i, 0)),
            pl.BlockSpec((1, tm, LANES), lambda b, i, j: (b, i, 0)),
        ],
        out_shape=[
            jax.ShapeDtypeStruct((B, N // LANES, S, LANES), F32),
            jax.ShapeDtypeStruct((B, S, LANES), F32),
        ],
        scratch_shapes=[pltpu.VMEM((tm, D), BF16)],
        compiler_params=_params(3),
        name="inproj",
    )(x, gain, w_main, w_ba)


def _gdn_prep_kernel(ba_ref, alog_ref, dtb_ref, col_ref, row_ref, *, n_heads):
    ba = ba_ref[0]
    S = ba.shape[0]
    g = -jnp.exp(alog_ref[...]) * _softplus(ba + dtb_ref[...])
    r = lax.broadcasted_iota(jnp.int32, (LANES, LANES), 0)
    c = lax.broadcasted_iota(jnp.int32, (LANES, LANES), 1)
    tril = jnp.where((r // GDN_CHUNK == c // GDN_CHUNK) & (c <= r), 1.0, 0.0).astype(F32)
    parts = [
        jnp.dot(tril, g[i * LANES:(i + 1) * LANES], precision=lax.Precision.HIGHEST, preferred_element_type=F32)
        for i in range(S // LANES)
    ]
    gc = jnp.concatenate(parts, axis=0)
    lane = lax.broadcasted_iota(jnp.int32, ba.shape, 1)
    col = jnp.where(lane < n_heads, _sigmoid(ba), gc)
    col_ref[0] = col
    row_ref[0] = col.T


def _gdn_prep(ba, alog_l, dtb_l, *, n_heads):
    B, S, _ = ba.shape
    return pl.pallas_call(
        functools.partial(_gdn_prep_kernel, n_heads=n_heads),
        grid=(B,),
        in_specs=[
            pl.BlockSpec((1, S, LANES), lambda b: (b, 0, 0)),
            pl.BlockSpec((1, LANES), lambda b: (0, 0)),
            pl.BlockSpec((1, LANES), lambda b: (0, 0)),
        ],
        out_specs=[
            pl.BlockSpec((1, S, LANES), lambda b: (b, 0, 0)),
            pl.BlockSpec((1, LANES, S), lambda b: (b, 0, 0)),
        ],
        out_shape=[
            jax.ShapeDtypeStruct((B, S, LANES), F32),
            jax.ShapeDtypeStruct((B, LANES, S), F32),
        ],
        compiler_params=_params(1),
        name="gdn_prep",
    )(ba, alog_l, dtb_l)


def _gdn_kernel(q_ref, k_ref, v_ref, z_ref, cwq_ref, cwk_ref, cwv_ref, col_ref, row_ref, gain_ref, o_ref,
                beta_s, gc_s, u_s, wq_s, kd_s, a_s, egl_s, *, n_heads, chunks_per_iter):
    C = GDN_CHUNK
    S = col_ref.shape[1]
    n_chunks = S // C
    h = pl.program_id(1)

    colv = col_ref[0]
    lane = lax.broadcasted_iota(jnp.int32, colv.shape, 1)
    beta_s[...] = jnp.broadcast_to(
        jnp.sum(jnp.where(lane == h, colv, 0.0), axis=1, keepdims=True), colv.shape)
    gc_s[...] = jnp.broadcast_to(
        jnp.sum(jnp.where(lane == h + n_heads, colv, 0.0), axis=1, keepdims=True), colv.shape)

    ri = lax.broadcasted_iota(jnp.int32, (C, C), 0)
    ci = lax.broadcasted_iota(jnp.int32, (C, C), 1)
    tril = ci <= ri
    strict = ci < ri

    def conv_silu(x_ref, cw_ref, n, r0):
        main = x_ref[0, 0, pl.ds(r0, C), :]
        halo = x_ref[0, 0, pl.ds(pl.multiple_of(jnp.maximum(r0 - 8, 0), 8), 8), :]
        halo = jnp.where(n > 0, halo, 0.0)
        xc = jnp.concatenate([halo, main], axis=0)
        cw = cw_ref[...]
        y = cw[3:4] * xc[8:8 + C]
        for i in range(GDN_CONV - 1):
            y = y + cw[i:i + 1] * xc[5 + i:5 + i + C]
        return y * _sigmoid(y)

    def l2n(x):
        return x * lax.rsqrt(jnp.sum(x * x, axis=-1, keepdims=True) + L2_EPS)

    def prep_chunk(n):
        r0 = pl.multiple_of(n * C, C)
        q = l2n(conv_silu(q_ref, cwq_ref, n, r0)) * (HEAD_DIM ** -0.5)
        k = l2n(conv_silu(k_ref, cwk_ref, n, r0))
        v = conv_silu(v_ref, cwv_ref, n, r0)
        beta = beta_s[pl.ds(r0, C), :]
        gcb = gc_s[pl.ds(r0, C), :]
        gc_row = row_ref[0, 0, pl.ds(n, 1), :]
        decay = jnp.exp(jnp.where(tril, gcb[:, :C] - gc_row, NEG_BIG))
        k_beta = k * beta
        kb = k.astype(BF16)
        qk = _dot_nt(jnp.concatenate([q, k_beta], axis=0).astype(BF16), kb)
        intra = jnp.where(tril, qk[:C] * decay, 0.0)
        lower = jnp.where(strict, qk[C:] * decay, 0.0)
        eg = jnp.exp(gcb)
        x = jnp.concatenate([v * beta, k_beta * eg], axis=1)
        m = -lower
        n_steps = C.bit_length() - 1
        for j in range(n_steps):
            mb = m.astype(BF16)
            x = x + _dot(mb, x.astype(BF16))
            if j + 1 < n_steps:
                m = _dot(mb, mb)
        gl = gcb[C - 1:C, :]
        u_s[pl.ds(r0, C), :] = x[:, :HEAD_DIM]
        wq_s[pl.ds(2 * r0, C), :] = x[:, HEAD_DIM:].astype(BF16)
        wq_s[pl.ds(2 * r0 + C, C), :] = (q * eg).astype(BF16)
        kd_s[pl.ds(r0, C), :] = (k * jnp.exp(gl - gcb)).astype(BF16)
        a_s[pl.ds(r0, C), :] = intra.astype(BF16)
        egl_s[pl.ds(pl.multiple_of(n * 8, 8), 8), :] = jnp.broadcast_to(jnp.exp(gl), (8, HEAD_DIM))

    def prep_body(it, carry):
        for s in range(chunks_per_iter):
            prep_chunk(it * chunks_per_iter + s)
        return carry

    lax.fori_loop(0, n_chunks // chunks_per_iter, prep_body, 0)

    gain = gain_ref[...]

    def step(n, state):
        r0 = pl.multiple_of(n * C, C)
        sb = state.astype(BF16)
        ws = _dot(wq_s[pl.ds(2 * r0, 2 * C), :], sb)
        v_new = u_s[pl.ds(r0, C), :] - ws[:C]
        vb = v_new.astype(BF16)
        o = ws[C:] + _dot(a_s[pl.ds(r0, C), :], vb)
        egl = egl_s[pl.ds(pl.multiple_of(n * 8, 8), 1), :]
        state = state * jnp.broadcast_to(egl, state.shape) + _dot_tn(kd_s[pl.ds(r0, C), :], vb)
        z = z_ref[0, 0, pl.ds(r0, C), :]
        o_ref[0, pl.ds(r0, C), :] = (_rms(o, gain) * (z * _sigmoid(z))).astype(o_ref.dtype)
        return state

    lax.fori_loop(0, n_chunks, step, jnp.zeros((HEAD_DIM, HEAD_DIM), F32))


def _gdn(proj, conv_w, col, row, o_gain, *, n_heads, chunks_per_iter=4):
    B, _, S, _ = proj.shape
    H = n_heads
    n_chunks = S // GDN_CHUNK

    def slab(off):
        return pl.BlockSpec((1, 1, S, LANES), lambda b, h: (b, off + h, 0, 0))

    def cw(off):
        return pl.BlockSpec((GDN_CONV, LANES), lambda b, h: (0, off + h))

    return pl.pallas_call(
        functools.partial(_gdn_kernel, n_heads=H, chunks_per_iter=chunks_per_iter),
        grid=(B, H),
        in_specs=[
            slab(0), slab(H), slab(2 * H), slab(3 * H),
            cw(0), cw(H), cw(2 * H),
            pl.BlockSpec((1, S, LANES), lambda b, h: (b, 0, 0)),
            pl.BlockSpec((1, 1, n_chunks, GDN_CHUNK), lambda b, h: (b, H + h, 0, 0)),
            pl.BlockSpec((1, LANES), lambda b, h: (0, 0)),
        ],
        out_specs=pl.BlockSpec((1, S, LANES), lambda b, h: (b, 0, h)),
        out_shape=jax.ShapeDtypeStruct((B, S, H * HEAD_DIM), BF16),
        scratch_shapes=[
            pltpu.VMEM((S, LANES), F32),
            pltpu.VMEM((S, LANES), F32),
            pltpu.VMEM((S, HEAD_DIM), F32),
            pltpu.VMEM((2 * S, HEAD_DIM), BF16),
            pltpu.VMEM((S, HEAD_DIM), BF16),
            pltpu.VMEM((S, GDN_CHUNK), BF16),
            pltpu.VMEM((n_chunks * 8, HEAD_DIM), F32),
        ],
        compiler_params=_params(2),
        name="gdn",
    )(proj, proj, proj, proj, conv_w, conv_w, conv_w, col, row, o_gain)


def _sb_kernel(q_ref, k_ref, v_ref, qg_ref, kg_ref, og_ref, o_ref, qs, ks, vs):
    T = SB_BLOCK
    S = q_ref.shape[2]
    n_blk = S // T
    qs[...] = _rms(q_ref[0, 0], qg_ref[...]).astype(BF16)
    ks[...] = _rms(k_ref[0, 0], kg_ref[...]).astype(BF16)
    vs[...] = v_ref[0, 0].astype(BF16)
    scale = HEAD_DIM ** -0.5
    og = og_ref[...]

    ri = lax.broadcasted_iota(jnp.int32, (T, T), 0)
    ci = lax.broadcasted_iota(jnp.int32, (T, T), 1)
    past = ci < ri
    jj = lax.broadcasted_iota(jnp.int32, (T, 2 * T), 0)
    ss = lax.broadcasted_iota(jnp.int32, (T, 2 * T), 1)
    uneg = jnp.where((ss >= T) | (jj >= ss), -1.0, 0.0).astype(BF16)

    def block(qb, kj, carry, acc, diag):
        c0 = pl.multiple_of(kj * T, T)
        z = _dot_nt(qb, ks[pl.ds(c0, T), :]) * scale
        sp = _softplus(z)
        if diag:
            sp = jnp.where(past, sp, 0.0)
        hi = sp.astype(BF16)
        lo = (sp - hi.astype(F32)).astype(BF16)
        cs = _dot(jnp.concatenate([hi, lo], axis=0), uneg)
        comb = cs[:T] + cs[T:]
        logw = z + carry + comb[:, :T]
        if diag:
            logw = jnp.where(past, logw, NEG_BIG)
        acc = acc + _dot(jnp.exp(logw).astype(BF16), vs[pl.ds(c0, T), :])
        return carry + comb[:, T:], acc

    def q_loop(qi, _):
        r0 = pl.multiple_of(qi * T, T)
        qb = qs[pl.ds(r0, T), :]
        zeros = jnp.zeros((T, T), F32)
        carry, acc = block(qb, qi, zeros, zeros, True)

        def k_loop(t, ca):
            return block(qb, qi - 1 - t, ca[0], ca[1], False)

        carry, acc = lax.fori_loop(0, qi, k_loop, (carry, acc))
        o_ref[0, pl.ds(r0, T), :] = _rms(acc, og).astype(o_ref.dtype)
        return 0

    lax.fori_loop(0, n_blk, q_loop, 0)


def _sb(proj, q_gain, k_gain, o_gain, *, n_heads, slab0):
    B, _, S, _ = proj.shape
    H = n_heads

    def slab(off):
        return pl.BlockSpec((1, 1, S, LANES), lambda b, h: (b, off + h, 0, 0))

    gain = pl.BlockSpec((1, LANES), lambda b, h: (0, 0))
    return pl.pallas_call(
        _sb_kernel,
        grid=(B, H),
        in_specs=[slab(slab0), slab(slab0 + H), slab(slab0 + 2 * H), gain, gain, gain],
        out_specs=pl.BlockSpec((1, S, LANES), lambda b, h: (b, 0, h)),
        out_shape=jax.ShapeDtypeStruct((B, S, H * HEAD_DIM), BF16),
        scratch_shapes=[pltpu.VMEM((S, HEAD_DIM), BF16)] * 3,
        compiler_params=_params(2),
        name="sb",
    )(proj, proj, proj, q_gain, k_gain, o_gain)


def _outproj_kernel(yg_ref, ys_ref, w_ref, x_ref, o_ref):
    kg = yg_ref.shape[2]
    o_ref[0] = x_ref[0] + _dot(yg_ref[0], w_ref[:kg, :]) + _dot(ys_ref[0], w_ref[kg:, :])


def _outproj(yg, ys, w, x, *, tm, tn):
    B, S, D = x.shape
    Kg, Ks = yg.shape[2], ys.shape[2]
    return pl.pallas_call(
        _outproj_kernel,
        grid=(B, S // tm, D // tn),
        in_specs=[
            pl.BlockSpec((1, tm, Kg), lambda b, i, j: (b, i, 0)),
            pl.BlockSpec((1, tm, Ks), lambda b, i, j: (b, i, 0)),
            pl.BlockSpec((Kg + Ks, tn), lambda b, i, j: (0, j)),
            pl.BlockSpec((1, tm, tn), lambda b, i, j: (b, i, j)),
        ],
        out_specs=pl.BlockSpec((1, tm, tn), lambda b, i, j: (b, i, j)),
        out_shape=jax.ShapeDtypeStruct((B, S, D), F32),
        compiler_params=_params(3),
        name="outproj",
    )(yg, ys, w, x)


def _ffn_up_kernel(x_ref, halo_ref, g_ref, wg_ref, wv_ref, cg_ref, cv_ref, bg_ref, bv_ref, o_ref,
                   h_scr, ug_scr, uv_scr):
    tm = x_ref.shape[1]

    @pl.when(pl.program_id(2) == 0)
    def _():
        gain = g_ref[...]
        hh = _rms(halo_ref[0], gain)
        hh = jnp.where(pl.program_id(1) > 0, hh, 0.0)
        h_scr[:HALO, :] = hh.astype(BF16)
        h_scr[HALO:, :] = _rms(x_ref[0], gain).astype(BF16)

    hb = h_scr[...]
    ug_scr[...] = _dot(hb, wg_ref[...])
    uv_scr[...] = _dot(hb, wv_ref[...])

    def conv(u_scr, c_ref, b_ref):
        cw = c_ref[...]
        y = b_ref[...] + cw[FFN_CONV - 1:FFN_CONV] * u_scr[pl.ds(HALO, tm), :]
        for i in range(FFN_CONV - 1):
            y = y + cw[i:i + 1] * u_scr[pl.ds(HALO - (FFN_CONV - 1) + i, tm), :]
        return y

    gate = conv(ug_scr, cg_ref, bg_ref)
    val = conv(uv_scr, cv_ref, bv_ref)
    o_ref[0] = (gate * _sigmoid(gate) * val).astype(o_ref.dtype)


def _ffn_up(x, gain, w_up, conv_w, bias, *, tm, tn):
    B, S, D = x.shape
    F = w_up.shape[1] // 2
    nf = F // tn
    hb = tm // HALO
    return pl.pallas_call(
        _ffn_up_kernel,
        grid=(B, S // tm, nf),
        in_specs=[
            pl.BlockSpec((1, tm, D), lambda b, i, j: (b, i, 0)),
            pl.BlockSpec((1, HALO, D), lambda b, i, j: (b, jnp.maximum(i * hb - 1, 0), 0)),
            pl.BlockSpec((1, D), lambda b, i, j: (0, 0)),
            pl.BlockSpec((D, tn), lambda b, i, j: (0, j)),
            pl.BlockSpec((D, tn), lambda b, i, j: (0, nf + j)),
            pl.BlockSpec((FFN_CONV, tn), lambda b, i, j: (0, j)),
            pl.BlockSpec((FFN_CONV, tn), lambda b, i, j: (0, nf + j)),
            pl.BlockSpec((1, tn), lambda b, i, j: (0, j)),
            pl.BlockSpec((1, tn), lambda b, i, j: (0, nf + j)),
        ],
        out_specs=pl.BlockSpec((1, tm, tn), lambda b, i, j: (b, i, j)),
        out_shape=jax.ShapeDtypeStruct((B, S, F), BF16),
        scratch_shapes=[
            pltpu.VMEM((tm + HALO, D), BF16),
            pltpu.VMEM((tm + HALO, tn), F32),
            pltpu.VMEM((tm + HALO, tn), F32),
        ],
        compiler_params=_params(3),
        name="ffn_up",
    )(x, x, gain, w_up, w_up, conv_w, conv_w, bias, bias)


def _ffn_down_kernel(a_ref, w_ref, x_ref, o_ref):
    o_ref[0] = x_ref[0] + _dot(a_ref[0], w_ref[...])


def _ffn_down(act, w, x, *, tm, tn):
    B, S, D = x.shape
    F = act.shape[2]
    return pl.pallas_call(
        _ffn_down_kernel,
        grid=(B, S // tm, D // tn),
        in_specs=[
            pl.BlockSpec((1, tm, F), lambda b, i, j: (b, i, 0)),
            pl.BlockSpec((F, tn), lambda b, i, j: (0, j)),
            pl.BlockSpec((1, tm, tn), lambda b, i, j: (b, i, j)),
        ],
        out_specs=pl.BlockSpec((1, tm, tn), lambda b, i, j: (b, i, j)),
        out_shape=jax.ShapeDtypeStruct((B, S, D), F32),
        compiler_params=_params(3),
        name="ffn_down",
    )(act, w, x)


def _tile(n, pref):
    t = min(n, pref)
    while n % t:
        t //= 2
    return t


def kernel(x, attn_norm, w_in, gdn_conv, gdn_a_log, gdn_dt_bias, gdn_o_norm, sb_q_norm, sb_k_norm, sb_o_norm,
           w_out, ffn_norm, w_up, ffn_conv, ffn_conv_bias, w_down):
    B, S, D = x.shape
    depth = w_in.shape[0]
    H = gdn_a_log.shape[1]
    W = H * HEAD_DIM
    assert w_in.shape[2] == 7 * W + 2 * H and S % LANES == 0 and 2 * H <= LANES
    F = w_down.shape[1]
    tm = _tile(S, 1024)
    tn_in = _tile(7 * W, 512)
    tn_d = _tile(D, 512)
    tn_f = _tile(F, 512)

    def lane_row(v, off):
        return jnp.zeros((1, LANES), F32).at[0, off:off + v.shape[0]].set(v)

    for l in range(depth):
        wl = w_in[l]
        w_main = jnp.concatenate([wl[:, :4 * W], wl[:, 4 * W + 2 * H:]], axis=1).astype(BF16)
        w_ba = jnp.pad(wl[:, 4 * W:4 * W + 2 * H], ((0, 0), (0, LANES - 2 * H))).astype(BF16)
        proj, ba = _inproj(x, attn_norm[l][None], w_main, w_ba, tm=tm, tn=tn_in)
        col, row = _gdn_prep(ba, lane_row(gdn_a_log[l], H), lane_row(gdn_dt_bias[l], H), n_heads=H)
        row = row.reshape(B, LANES, S // GDN_CHUNK, GDN_CHUNK)
        y_gdn = _gdn(proj, gdn_conv[l], col, row, gdn_o_norm[l][None], n_heads=H)
        y_sb = _sb(proj, sb_q_norm[l][None], sb_k_norm[l][None], sb_o_norm[l][None], n_heads=H, slab0=4 * H)
        x = _outproj(y_gdn, y_sb, w_out[l].astype(BF16), x, tm=tm, tn=tn_d)
        act = _ffn_up(x, ffn_norm[l][None], w_up[l].astype(BF16), ffn_conv[l], ffn_conv_bias[l][None],
                      tm=tm, tn=tn_f)
        x = _ffn_down(act, w_down[l].astype(BF16), x, tm=tm, tn=tn_d)
    return x
```

```python
import functools

import jax
import jax.numpy as jnp
from jax import lax
from jax.experimental import pallas as pl
from jax.experimental.pallas import tpu as pltpu

F32 = jnp.float32
BF16 = jnp.bfloat16
RMS_EPS = 1e-6
L2_EPS = 1e-6
HEAD_DIM = 128
LANES = 128
GDN_CHUNK = 64
GDN_CONV = 4
FFN_CONV = 3
SB_BLOCK = 128
HALO = 16
NEG_BIG = -1e30
LOG2E = 1.4426950408889634
VMEM_LIMIT = 56 * 1024 * 1024


def _dot(a, b):
    return jnp.dot(a, b, preferred_element_type=F32)


def _dot_nt(a, b):
    return lax.dot_general(a, b, (((1,), (1,)), ((), ())), preferred_element_type=F32)


def _dot_tn(a, b):
    return lax.dot_general(a, b, (((0,), (0,)), ((), ())), preferred_element_type=F32)


def _sigmoid(x):
    return 1.0 / (1.0 + jnp.exp(-x))


def _softplus(x):
    return jnp.maximum(x, 0.0) + jnp.log1p(jnp.exp(-jnp.abs(x)))


def _rms(x, gain):
    return (x * lax.rsqrt(jnp.mean(x * x, axis=-1, keepdims=True) + RMS_EPS)) * gain


def _params(n_axes):
    return pltpu.CompilerParams(dimension_semantics=("arbitrary",) * n_axes, vmem_limit_bytes=VMEM_LIMIT)


def _inproj_kernel(x_ref, g_ref, w_ref, wba_ref, o_ref, ba_ref, h_scr, *, n_slabs):
    @pl.when(pl.program_id(2) == 0)
    def _():
        hb = _rms(x_ref[0], g_ref[...]).astype(BF16)
        h_scr[...] = hb
        ba_ref[0] = _dot(hb, wba_ref[...])

    res = _dot(h_scr[...], w_ref[...])
    for c in range(n_slabs):
        o_ref[0, c] = res[:, c * LANES:(c + 1) * LANES]


def _inproj(x, gain, w_main, w_ba, *, tm, tn):
    B, S, D = x.shape
    N = w_main.shape[1]
    n_slabs = tn // LANES
    return pl.pallas_call(
        functools.partial(_inproj_kernel, n_slabs=n_slabs),
        grid=(B, S // tm, N // tn),
        in_specs=[
            pl.BlockSpec((1, tm, D), lambda b, i, j: (b, i, 0)),
            pl.BlockSpec((1, D), lambda b, i, j: (0, 0)),
            pl.BlockSpec((D, tn), lambda b, i, j: (0, j)),
            pl.BlockSpec((D, LANES), lambda b, i, j: (0, 0)),
        ],
        out_specs=[
            pl.BlockSpec((1, n_slabs, tm, LANES), lambda b, i, j: (b, j, i, 0)),
            pl.BlockSpec((1, tm, LANES), lambda b, i, j: (b, i, 0)),
        ],
        out_shape=[
            jax.ShapeDtypeStruct((B, N // LANES, S, LANES), F32),
            jax.ShapeDtypeStruct((B, S, LANES), F32),
        ],
        scratch_shapes=[pltpu.VMEM((tm, D), BF16)],
        compiler_params=_params(3),
        name="inproj",
    )(x, gain, w_main, w_ba)


def _gdn_prep_kernel(ba_ref, alog_ref, dtb_ref, col_ref, row_ref, *, n_heads):
    ba = ba_ref[0]
    S = ba.shape[0]
    g = -jnp.exp(alog_ref[...]) * _softplus(ba + dtb_ref[...])
    r = lax.broadcasted_iota(jnp.int32, (LANES, LANES), 0)
    c = lax.broadcasted_iota(jnp.int32, (LANES, LANES), 1)
    tril = jnp.where((r // GDN_CHUNK == c // GDN_CHUNK) & (c <= r), 1.0, 0.0).astype(F32)
    parts = [
        jnp.dot(tril, g[i * LANES:(i + 1) * LANES], precision=lax.Precision.HIGHEST, preferred_element_type=F32)
        for i in range(S // LANES)
    ]
    gc = jnp.concatenate(parts, axis=0)
    lane = lax.broadcasted_iota(jnp.int32, ba.shape, 1)
    col = jnp.where(lane < n_heads, _sigmoid(ba), gc)
    col_ref[0] = col
    row_ref[0] = col.T


def _gdn_prep(ba, alog_l, dtb_l, *, n_heads):
    B, S, _ = ba.shape
    return pl.pallas_call(
        functools.partial(_gdn_prep_kernel, n_heads=n_heads),
        grid=(B,),
        in_specs=[
            pl.BlockSpec((1, S, LANES), lambda b: (b, 0, 0)),
            pl.BlockSpec((1, LANES), lambda b: (0, 0)),
            pl.BlockSpec((1, LANES), lambda b: (0, 0)),
        ],
        out_specs=[
            pl.BlockSpec((1, S, LANES), lambda b: (b, 0, 0)),
            pl.BlockSpec((1, LANES, S), lambda b: (b, 0, 0)),
        ],
        out_shape=[
            jax.ShapeDtypeStruct((B, S, LANES), F32),
            jax.ShapeDtypeStruct((B, LANES, S), F32),
        ],
        compiler_params=_params(1),
        name="gdn_prep",
    )(ba, alog_l, dtb_l)


def _gdn_kernel(q_ref, k_ref, v_ref, z_ref, cwq_ref, cwk_ref, cwv_ref, col_ref, row_ref, gain_ref, o_ref,
                beta_s, gc_s, pg_s, qn_s, on_s, egl_s, *, n_heads, heads_per_step, chunks_per_iter):
    C = GDN_CHUNK
    S = col_ref.shape[1]
    n_chunks = S // C
    hp = heads_per_step
    h0 = pl.program_id(1) * hp

    colv = col_ref[0]
    lane = lax.broadcasted_iota(jnp.int32, colv.shape, 1)
    for e in range(hp):
        beta_s[e] = jnp.broadcast_to(
            jnp.sum(jnp.where(lane == h0 + e, colv, 0.0), axis=1, keepdims=True), colv.shape)
        gc_s[e] = jnp.broadcast_to(
            jnp.sum(jnp.where(lane == h0 + e + n_heads, colv, 0.0), axis=1, keepdims=True), colv.shape)

    ri = lax.broadcasted_iota(jnp.int32, (C, C), 0)
    ci = lax.broadcasted_iota(jnp.int32, (C, C), 1)
    tril = ci <= ri
    strict = ci < ri

    def conv_silu(x_ref, cw_ref, e, n, r0):
        main = x_ref[0, e, pl.ds(r0, C), :]
        halo = x_ref[0, e, pl.ds(pl.multiple_of(jnp.maximum(r0 - 8, 0), 8), 8), :]
        halo = jnp.where(n > 0, halo, 0.0)
        xc = jnp.concatenate([halo, main], axis=0)
        cw = cw_ref[:, e * LANES:(e + 1) * LANES]
        y = cw[3:4] * xc[8:8 + C]
        for i in range(GDN_CONV - 1):
            y = y + cw[i:i + 1] * xc[5 + i:5 + i + C]
        return y * _sigmoid(y)

    def l2n(x):
        return x * lax.rsqrt(jnp.sum(x * x, axis=-1, keepdims=True) + L2_EPS)

    def prep_body(e, it):
        ns = [it * chunks_per_iter + s for s in range(chunks_per_iter)]
        r0s = [pl.multiple_of(n * C, C) for n in ns]
        q = [l2n(conv_silu(q_ref, cwq_ref, e, n, r0)) * (HEAD_DIM ** -0.5) for n, r0 in zip(ns, r0s)]
        k = [l2n(conv_silu(k_ref, cwk_ref, e, n, r0)) for n, r0 in zip(ns, r0s)]
        v = [conv_silu(v_ref, cwv_ref, e, n, r0) for n, r0 in zip(ns, r0s)]
        beta = [beta_s[e, pl.ds(r0, C), :] for r0 in r0s]
        gcb = [gc_s[e, pl.ds(r0, C), :] for r0 in r0s]
        decay = [jnp.exp(jnp.where(tril, g[:, :C] - row_ref[0, e, pl.ds(n, 1), :], NEG_BIG))
                 for n, g in zip(ns, gcb)]
        k_beta = [a * b for a, b in zip(k, beta)]
        qk = [_dot_nt(jnp.concatenate([a, b], axis=0).astype(BF16), c.astype(BF16))
              for a, b, c in zip(q, k_beta, k)]
        intra = [jnp.where(tril, a[:C] * d, 0.0).astype(BF16) for a, d in zip(qk, decay)]
        eg = [jnp.exp(g) for g in gcb]
        x = [jnp.concatenate([a * b, c * d], axis=1) for a, b, c, d in zip(v, beta, k_beta, eg)]
        m = [jnp.where(strict, -(a[C:] * d), 0.0) for a, d in zip(qk, decay)]
        n_steps = C.bit_length() - 1
        for j in range(n_steps):
            mb = [a.astype(BF16) for a in m]
            x = [a + _dot(b, a.astype(BF16)) for a, b in zip(x, mb)]
            if j + 1 < n_steps:
                m = [_dot(b, b) for b in mb]
        gl = [g[C - 1:C, :] for g in gcb]
        xb = [a.astype(BF16) for a in x]
        kd = [(a * jnp.exp(l - g)).astype(BF16) for a, l, g in zip(k, gl, gcb)]
        kuw = [_dot_tn(a, b) for a, b in zip(kd, xb)]
        auw = [_dot(a, b) for a, b in zip(intra, xb)]
        for i, n in enumerate(ns):
            p0 = pl.multiple_of(n * (HEAD_DIM + C), HEAD_DIM + C)
            pg_s[e, pl.ds(p0, HEAD_DIM), :] = kuw[i][:, HEAD_DIM:].astype(BF16)
            pg_s[e, pl.ds(p0 + HEAD_DIM, C), :] = (q[i] * eg[i] - auw[i][:, HEAD_DIM:]).astype(BF16)
            qn_s[e, pl.ds(pl.multiple_of(n * HEAD_DIM, HEAD_DIM), HEAD_DIM), :] = kuw[i][:, :HEAD_DIM]
            on_s[e, pl.ds(r0s[i], C), :] = auw[i][:, :HEAD_DIM]
            egl_s[e, pl.ds(pl.multiple_of(n * 8, 8), 8), :] = jnp.broadcast_to(jnp.exp(gl[i]), (8, HEAD_DIM))

    for e in range(hp):
        def body(it, carry, e=e):
            prep_body(e, it)
            return carry

        lax.fori_loop(0, n_chunks // chunks_per_iter, body, 0)

    gain = gain_ref[...]

    def step(n, states):
        r0 = pl.multiple_of(n * C, C)
        p0 = pl.multiple_of(n * (HEAD_DIM + C), HEAD_DIM + C)
        n0 = pl.multiple_of(n * HEAD_DIM, HEAD_DIM)
        ps = [_dot(pg_s[e, pl.ds(p0, HEAD_DIM + C), :], states[e].astype(BF16))
              for e in range(hp)]
        new_states = []
        for e in range(hp):
            egl = egl_s[e, pl.ds(pl.multiple_of(n * 8, 8), 1), :]
            new_states.append(states[e] * jnp.broadcast_to(egl, states[e].shape)
                              + (qn_s[e, pl.ds(n0, HEAD_DIM), :] - ps[e][:HEAD_DIM]))
        for e in range(hp):
            o = ps[e][HEAD_DIM:] + on_s[e, pl.ds(r0, C), :]
            z = z_ref[0, e, pl.ds(r0, C), :]
            o_ref[0, pl.ds(r0, C), e * LANES:(e + 1) * LANES] = (
                _rms(o, gain) * (z * _sigmoid(z))).astype(o_ref.dtype)
        return tuple(new_states)

    lax.fori_loop(0, n_chunks, step, tuple(jnp.zeros((HEAD_DIM, HEAD_DIM), F32) for _ in range(hp)))


def _gdn(proj, conv_w, col, row, o_gain, *, n_heads, chunks_per_iter=8):
    B, _, S, _ = proj.shape
    H = n_heads
    hp = 2 if H % 2 == 0 else 1
    n_chunks = S // GDN_CHUNK

    def slab(off):
        return pl.BlockSpec((1, hp, S, LANES), lambda b, h: (b, off // hp + h, 0, 0))

    def cw(off):
        return pl.BlockSpec((GDN_CONV, hp * LANES), lambda b, h: (0, off // hp + h))

    def per_head(rows, dtype):
        return pltpu.VMEM((hp, rows, HEAD_DIM), dtype)

    return pl.pallas_call(
        functools.partial(_gdn_kernel, n_heads=H, heads_per_step=hp,
                          chunks_per_iter=_tile(n_chunks, chunks_per_iter)),
        grid=(B, H // hp),
        in_specs=[
            slab(0), slab(H), slab(2 * H), slab(3 * H),
            cw(0), cw(H), cw(2 * H),
            pl.BlockSpec((1, S, LANES), lambda b, h: (b, 0, 0)),
            pl.BlockSpec((1, hp, n_chunks, GDN_CHUNK), lambda b, h: (b, H // hp + h, 0, 0)),
            pl.BlockSpec((1, LANES), lambda b, h: (0, 0)),
        ],
        out_specs=pl.BlockSpec((1, S, hp * LANES), lambda b, h: (b, 0, h)),
        out_shape=jax.ShapeDtypeStruct((B, S, H * HEAD_DIM), BF16),
        scratch_shapes=[
            per_head(S, F32),
            per_head(S, F32),
            per_head(n_chunks * (HEAD_DIM + GDN_CHUNK), BF16),
            per_head(n_chunks * HEAD_DIM, F32),
            per_head(S, F32),
            per_head(n_chunks * 8, F32),
        ],
        compiler_params=_params(2),
        name="gdn",
    )(proj, proj, proj, proj, conv_w, conv_w, conv_w, col, row, o_gain)


def _sb_kernel(q_ref, k_ref, v_ref, qg_ref, kg_ref, og_ref, o_ref, qs, ks, vs, carry_s, acc_s, *, group):
    T = SB_BLOCK
    G = group
    Q = G * T
    S = q_ref.shape[2]
    qs[...] = _rms(q_ref[0, 0], qg_ref[...]).astype(BF16)
    ks[...] = _rms(k_ref[0, 0], kg_ref[...]).astype(BF16)
    vs[...] = v_ref[0, 0].astype(BF16)
    scale = HEAD_DIM ** -0.5 * LOG2E
    og = og_ref[...]
    sign_bit = jnp.int32(-2 ** 31)

    ri = lax.broadcasted_iota(jnp.int32, (T, T), 0)
    ci = lax.broadcasted_iota(jnp.int32, (T, T), 1)
    past = ci < ri
    jj = lax.broadcasted_iota(jnp.int32, (2 * T, 2 * T), 0) % T
    ss = lax.broadcasted_iota(jnp.int32, (2 * T, 2 * T), 1)
    uneg = jnp.where((ss >= T) | (jj >= ss), -1.0, 0.0).astype(BF16)

    def big_tile(qb, k0, carry, diag):
        z = _dot_nt(qb, ks[pl.ds(k0, Q), :]) * scale
        zs, lhs = [], []
        for d in range(G):
            zd = z[d * T if diag else 0:, d * T:(d + 1) * T]
            neg_abs = lax.bitcast_convert_type(lax.bitcast_convert_type(zd, jnp.int32) | sign_bit, F32)
            sp = jnp.maximum(zd, 0.0) + jnp.log(1.0 + jnp.exp2(neg_abs)) * LOG2E
            if diag:
                top = jnp.where(past, sp[:T], 0.0)
                sp = top if d == G - 1 else jnp.concatenate([top, sp[T:]], axis=0)
            hi = sp.astype(BF16)
            lo = (sp - hi.astype(F32)).astype(BF16)
            zs.append(zd)
            lhs.append(jnp.concatenate([hi, lo], axis=1))
        comb = _dot(lhs[0] if G == 1 else jnp.concatenate(lhs, axis=0), uneg)
        offs = [0]
        for d in range(G):
            offs.append(offs[-1] + zs[d].shape[0])
        w_blocks = [None] * G
        for d in range(G - 1, -1, -1):
            lo_r = d * T if diag else 0
            cd = comb[offs[d]:offs[d + 1]]
            c_rows = carry[lo_r:]
            logw = zs[d] + c_rows + cd[:, :T]
            if diag:
                top = jnp.where(past, logw[:T], NEG_BIG)
                logw = top if d == G - 1 else jnp.concatenate([top, logw[T:]], axis=0)
            w = jnp.exp2(logw).astype(BF16)
            if lo_r:
                w = jnp.concatenate([jnp.zeros((lo_r, T), BF16), w], axis=0)
            w_blocks[d] = w
            c_new = c_rows + cd[:, T:]
            carry = c_new if lo_r == 0 else jnp.concatenate([carry[:lo_r], c_new], axis=0)
        wt = w_blocks[0] if G == 1 else jnp.concatenate(w_blocks, axis=1)
        return carry, _dot(wt, vs[pl.ds(k0, Q), :])

    def sup_loop(i, _):
        q0 = pl.multiple_of(i * Q, Q)
        qb = qs[pl.ds(q0, Q), :]
        carry, contrib = big_tile(qb, q0, jnp.zeros((Q, T), F32), True)
        carry_s[...] = carry
        acc_s[...] = contrib

        def k_loop(t, _):
            k0 = pl.multiple_of((i - 1 - t) * Q, Q)
            carry, contrib = big_tile(qb, k0, carry_s[...], False)
            carry_s[...] = carry
            acc_s[...] += contrib
            return 0

        lax.fori_loop(0, i, k_loop, 0)
        o_ref[0, pl.ds(q0, Q), :] = _rms(acc_s[...], og).astype(o_ref.dtype)
        return 0

    lax.fori_loop(0, S // Q, sup_loop, 0)


def _sb(proj, q_gain, k_gain, o_gain, *, n_heads, slab0):
    B, _, S, _ = proj.shape
    H = n_heads
    group = _tile(S // SB_BLOCK, 4)
    Q = group * SB_BLOCK

    def slab(off):
        return pl.BlockSpec((1, 1, S, LANES), lambda b, h: (b, off + h, 0, 0))

    gain = pl.BlockSpec((1, LANES), lambda b, h: (0, 0))
    return pl.pallas_call(
        functools.partial(_sb_kernel, group=group),
        grid=(B, H),
        in_specs=[slab(slab0), slab(slab0 + H), slab(slab0 + 2 * H), gain, gain, gain],
        out_specs=pl.BlockSpec((1, S, LANES), lambda b, h: (b, 0, h)),
        out_shape=jax.ShapeDtypeStruct((B, S, H * HEAD_DIM), BF16),
        scratch_shapes=[pltpu.VMEM((S, HEAD_DIM), BF16)] * 3
        + [pltpu.VMEM((Q, SB_BLOCK), F32), pltpu.VMEM((Q, HEAD_DIM), F32)],
        compiler_params=_params(2),
        name="sb",
    )(proj, proj, proj, q_gain, k_gain, o_gain)


def _outproj_kernel(yg_ref, ys_ref, w_ref, x_ref, o_ref):
    kg = yg_ref.shape[2]
    o_ref[0] = x_ref[0] + _dot(yg_ref[0], w_ref[:kg, :]) + _dot(ys_ref[0], w_ref[kg:, :])


def _outproj(yg, ys, w, x, *, tm, tn):
    B, S, D = x.shape
    Kg, Ks = yg.shape[2], ys.shape[2]
    return pl.pallas_call(
        _outproj_kernel,
        grid=(B, S // tm, D // tn),
        in_specs=[
            pl.BlockSpec((1, tm, Kg), lambda b, i, j: (b, i, 0)),
            pl.BlockSpec((1, tm, Ks), lambda b, i, j: (b, i, 0)),
            pl.BlockSpec((Kg + Ks, tn), lambda b, i, j: (0, j)),
            pl.BlockSpec((1, tm, tn), lambda b, i, j: (b, i, j)),
        ],
        out_specs=pl.BlockSpec((1, tm, tn), lambda b, i, j: (b, i, j)),
        out_shape=jax.ShapeDtypeStruct((B, S, D), F32),
        compiler_params=_params(3),
        name="outproj",
    )(yg, ys, w, x)


def _ffn_up_kernel(x_ref, halo_ref, g_ref, wg_ref, wv_ref, cg_ref, cv_ref, bg_ref, bv_ref, o_ref,
                   h_scr, ug_scr, uv_scr):
    tm = x_ref.shape[1]

    @pl.when(pl.program_id(2) == 0)
    def _():
        gain = g_ref[...]
        hh = _rms(halo_ref[0], gain)
        hh = jnp.where(pl.program_id(1) > 0, hh, 0.0)
        h_scr[:HALO, :] = hh.astype(BF16)
        h_scr[HALO:, :] = _rms(x_ref[0], gain).astype(BF16)

    hb = h_scr[...]
    ug_scr[...] = _dot(hb, wg_ref[...])
    uv_scr[...] = _dot(hb, wv_ref[...])

    def conv(u_scr, c_ref, b_ref):
        cw = c_ref[...]
        y = b_ref[...] + cw[FFN_CONV - 1:FFN_CONV] * u_scr[pl.ds(HALO, tm), :]
        for i in range(FFN_CONV - 1):
            y = y + cw[i:i + 1] * u_scr[pl.ds(HALO - (FFN_CONV - 1) + i, tm), :]
        return y

    gate = conv(ug_scr, cg_ref, bg_ref)
    val = conv(uv_scr, cv_ref, bv_ref)
    o_ref[0] = (gate * _sigmoid(gate) * val).astype(o_ref.dtype)


def _ffn_up(x, gain, w_up, conv_w, bias, *, tm, tn):
    B, S, D = x.shape
    F = w_up.shape[1] // 2
    nf = F // tn
    hb = tm // HALO
    return pl.pallas_call(
        _ffn_up_kernel,
        grid=(B, S // tm, nf),
        in_specs=[
            pl.BlockSpec((1, tm, D), lambda b, i, j: (b, i, 0)),
            pl.BlockSpec((1, HALO, D), lambda b, i, j: (b, jnp.maximum(i * hb - 1, 0), 0)),
            pl.BlockSpec((1, D), lambda b, i, j: (0, 0)),
            pl.BlockSpec((D, tn), lambda b, i, j: (0, j)),
            pl.BlockSpec((D, tn), lambda b, i, j: (0, nf + j)),
            pl.BlockSpec((FFN_CONV, tn), lambda b, i, j: (0, j)),
            pl.BlockSpec((FFN_CONV, tn), lambda b, i, j: (0, nf + j)),
            pl.BlockSpec((1, tn), lambda b, i, j: (0, j)),
            pl.BlockSpec((1, tn), lambda b, i, j: (0, nf + j)),
        ],
        out_specs=pl.BlockSpec((1, tm, tn), lambda b, i, j: (b, i, j)),
        out_shape=jax.ShapeDtypeStruct((B, S, F), BF16),
        scratch_shapes=[
            pltpu.VMEM((tm + HALO, D), BF16),
            pltpu.VMEM((tm + HALO, tn), F32),
            pltpu.VMEM((tm + HALO, tn), F32),
        ],
        compiler_params=_params(3),
        name="ffn_up",
    )(x, x, gain, w_up, w_up, conv_w, conv_w, bias, bias)


def _ffn_down_kernel(a_ref, w_ref, x_ref, o_ref):
    o_ref[0] = x_ref[0] + _dot(a_ref[0], w_ref[...])


def _ffn_down(act, w, x, *, tm, tn):
    B, S, D = x.shape
    F = act.shape[2]
    return pl.pallas_call(
        _ffn_down_kernel,
        grid=(B, S // tm, D // tn),
        in_specs=[
            pl.BlockSpec((1, tm, F), lambda b, i, j: (b, i, 0)),
            pl.BlockSpec((F, tn), lambda b, i, j: (0, j)),
            pl.BlockSpec((1, tm, tn), lambda b, i, j: (b, i, j)),
        ],
        out_specs=pl.BlockSpec((1, tm, tn), lambda b, i, j: (b, i, j)),
        out_shape=jax.ShapeDtypeStruct((B, S, D), F32),
        compiler_params=_params(3),
        name="ffn_down",
    )(act, w, x)


def _tile(n, pref):
    t = min(n, pref)
    while n % t:
        t //= 2
    return t


def kernel(x, attn_norm, w_in, gdn_conv, gdn_a_log, gdn_dt_bias, gdn_o_norm, sb_q_norm, sb_k_norm, sb_o_norm,
           w_out, ffn_norm, w_up, ffn_conv, ffn_conv_bias, w_down):
    B, S, D = x.shape
    depth = w_in.shape[0]
    H = gdn_a_log.shape[1]
    W = H * HEAD_DIM
    assert w_in.shape[2] == 7 * W + 2 * H and S % LANES == 0 and 2 * H <= LANES
    F = w_down.shape[1]
    tm = _tile(S, 1024)
    tn_in = _tile(7 * W, 1024)
    tn_d = _tile(D, 512)
    tn_o = _tile(D, 1024)
    tn_f = _tile(F, 512)

    def lane_row(v, off):
        return jnp.zeros((1, LANES), F32).at[0, off:off + v.shape[0]].set(v)

    for l in range(depth):
        wl = w_in[l]
        w_main = jnp.concatenate([wl[:, :4 * W], wl[:, 4 * W + 2 * H:]], axis=1).astype(BF16)
        w_ba = jnp.pad(wl[:, 4 * W:4 * W + 2 * H], ((0, 0), (0, LANES - 2 * H))).astype(BF16)
        proj, ba = _inproj(x, attn_norm[l][None], w_main, w_ba, tm=tm, tn=tn_in)
        col, row = _gdn_prep(ba, lane_row(gdn_a_log[l], H), lane_row(gdn_dt_bias[l], H), n_heads=H)
        row = row.reshape(B, LANES, S // GDN_CHUNK, GDN_CHUNK)
        y_gdn = _gdn(proj, gdn_conv[l], col, row, gdn_o_norm[l][None], n_heads=H)
        y_sb = _sb(proj, sb_q_norm[l][None], sb_k_norm[l][None], sb_o_norm[l][None], n_heads=H, slab0=4 * H)
        x = _outproj(y_gdn, y_sb, w_out[l].astype(BF16), x, tm=tm, tn=tn_o)
        act = _ffn_up(x, ffn_norm[l][None], w_up[l].astype(BF16), ffn_conv[l], ffn_conv_bias[l][None],
                      tm=tm, tn=tn_f)
        x = _ffn_down(act, w_down[l].astype(BF16), x, tm=tm, tn=tn_d)
    return x
```

```python
import functools

import jax
import jax.numpy as jnp
from jax import lax
from jax.experimental import pallas as pl
from jax.experimental.pallas import tpu as pltpu

F32 = jnp.float32
BF16 = jnp.bfloat16
RMS_EPS = 1e-6
L2_EPS = 1e-6
HEAD_DIM = 128
LANES = 128
GDN_CHUNK = 64
GDN_CONV = 4
FFN_CONV = 3
SB_BLOCK = 128
HALO = 16
NEG_BIG = -1e30
LOG2E = 1.4426950408889634
VMEM_LIMIT = 56 * 1024 * 1024


def _dot(a, b):
    return jnp.dot(a, b, preferred_element_type=F32)


def _dot_nt(a, b):
    return lax.dot_general(a, b, (((1,), (1,)), ((), ())), preferred_element_type=F32)


def _dot_tn(a, b):
    return lax.dot_general(a, b, (((0,), (0,)), ((), ())), preferred_element_type=F32)


def _sigmoid(x):
    return 1.0 / (1.0 + jnp.exp(-x))


def _softplus(x):
    return jnp.maximum(x, 0.0) + jnp.log1p(jnp.exp(-jnp.abs(x)))


def _rms(x, gain):
    return (x * lax.rsqrt(jnp.mean(x * x, axis=-1, keepdims=True) + RMS_EPS)) * gain


def _params(n_axes):
    return pltpu.CompilerParams(dimension_semantics=("arbitrary",) * n_axes, vmem_limit_bytes=VMEM_LIMIT)


def _inproj_kernel(x_ref, g_ref, w_ref, wba_ref, o_ref, ba_ref, h_scr, *, n_slabs):
    @pl.when(pl.program_id(2) == 0)
    def _():
        hb = _rms(x_ref[0], g_ref[...]).astype(BF16)
        h_scr[...] = hb
        ba_ref[0] = _dot(hb, wba_ref[...])

    res = _dot(h_scr[...], w_ref[...])
    for c in range(n_slabs):
        o_ref[0, c] = res[:, c * LANES:(c + 1) * LANES]


def _inproj(x, gain, w_main, w_ba, *, tm, tn):
    B, S, D = x.shape
    N = w_main.shape[1]
    n_slabs = tn // LANES
    return pl.pallas_call(
        functools.partial(_inproj_kernel, n_slabs=n_slabs),
        grid=(B, S // tm, N // tn),
        in_specs=[
            pl.BlockSpec((1, tm, D), lambda b, i, j: (b, i, 0)),
            pl.BlockSpec((1, D), lambda b, i, j: (0, 0)),
            pl.BlockSpec((D, tn), lambda b, i, j: (0, j)),
            pl.BlockSpec((D, LANES), lambda b, i, j: (0, 0)),
        ],
        out_specs=[
            pl.BlockSpec((1, n_slabs, tm, LANES), lambda b, i, j: (b, j, i, 0)),
            pl.BlockSpec((1, tm, LANES), lambda b, i, j: (b, i, 0)),
        ],
        out_shape=[
            jax.ShapeDtypeStruct((B, N // LANES, S, LANES), F32),
            jax.ShapeDtypeStruct((B, S, LANES), F32),
        ],
        scratch_shapes=[pltpu.VMEM((tm, D), BF16)],
        compiler_params=_params(3),
        name="inproj",
    )(x, gain, w_main, w_ba)


def _gdn_prep_kernel(ba_ref, alog_ref, dtb_ref, col_ref, row_ref, *, n_heads):
    ba = ba_ref[0]
    S = ba.shape[0]
    g = -jnp.exp(alog_ref[...]) * _softplus(ba + dtb_ref[...])
    r = lax.broadcasted_iota(jnp.int32, (LANES, LANES), 0)
    c = lax.broadcasted_iota(jnp.int32, (LANES, LANES), 1)
    tril = jnp.where((r // GDN_CHUNK == c // GDN_CHUNK) & (c <= r), 1.0, 0.0).astype(F32)
    parts = [
        jnp.dot(tril, g[i * LANES:(i + 1) * LANES], precision=lax.Precision.HIGHEST, preferred_element_type=F32)
        for i in range(S // LANES)
    ]
    gc = jnp.concatenate(parts, axis=0)
    lane = lax.broadcasted_iota(jnp.int32, ba.shape, 1)
    col = jnp.where(lane < n_heads, _sigmoid(ba), gc)
    col_ref[0] = col
    row_ref[0] = col.T


def _gdn_prep(ba, alog_l, dtb_l, *, n_heads):
    B, S, _ = ba.shape
    return pl.pallas_call(
        functools.partial(_gdn_prep_kernel, n_heads=n_heads),
        grid=(B,),
        in_specs=[
            pl.BlockSpec((1, S, LANES), lambda b: (b, 0, 0)),
            pl.BlockSpec((1, LANES), lambda b: (0, 0)),
            pl.BlockSpec((1, LANES), lambda b: (0, 0)),
        ],
        out_specs=[
            pl.BlockSpec((1, S, LANES), lambda b: (b, 0, 0)),
            pl.BlockSpec((1, LANES, S), lambda b: (b, 0, 0)),
        ],
        out_shape=[
            jax.ShapeDtypeStruct((B, S, LANES), F32),
            jax.ShapeDtypeStruct((B, LANES, S), F32),
        ],
        compiler_params=_params(1),
        name="gdn_prep",
    )(ba, alog_l, dtb_l)


def _gdn_kernel(q_ref, k_ref, v_ref, z_ref, cwq_ref, cwk_ref, cwv_ref, col_ref, row_ref, gain_ref, o_ref,
                beta_s, gc_s, pg_s, qn_s, on_s, egl_s, *, n_heads, heads_per_step, chunks_per_iter):
    C = GDN_CHUNK
    S = col_ref.shape[1]
    n_chunks = S // C
    hp = heads_per_step
    h0 = pl.program_id(1) * hp

    colv = col_ref[0]
    lane = lax.broadcasted_iota(jnp.int32, colv.shape, 1)
    for e in range(hp):
        beta_s[e] = jnp.broadcast_to(
            jnp.sum(jnp.where(lane == h0 + e, colv, 0.0), axis=1, keepdims=True), colv.shape)
        gc_s[e] = jnp.broadcast_to(
            jnp.sum(jnp.where(lane == h0 + e + n_heads, colv, 0.0), axis=1, keepdims=True), colv.shape)

    ri = lax.broadcasted_iota(jnp.int32, (C, C), 0)
    ci = lax.broadcasted_iota(jnp.int32, (C, C), 1)
    tril = ci <= ri
    strict = ci < ri

    def conv_silu(x_ref, cw_ref, e, n, r0):
        main = x_ref[0, e, pl.ds(r0, C), :]
        halo = x_ref[0, e, pl.ds(pl.multiple_of(jnp.maximum(r0 - 8, 0), 8), 8), :]
        halo = jnp.where(n > 0, halo, 0.0)
        xc = jnp.concatenate([halo, main], axis=0)
        cw = cw_ref[:, e * LANES:(e + 1) * LANES]
        y = cw[3:4] * xc[8:8 + C]
        for i in range(GDN_CONV - 1):
            y = y + cw[i:i + 1] * xc[5 + i:5 + i + C]
        return y * _sigmoid(y)

    def l2n(x):
        return x * lax.rsqrt(jnp.sum(x * x, axis=-1, keepdims=True) + L2_EPS)

    def prep_body(e, it):
        ns = [it * chunks_per_iter + s for s in range(chunks_per_iter)]
        r0s = [pl.multiple_of(n * C, C) for n in ns]
        q = [l2n(conv_silu(q_ref, cwq_ref, e, n, r0)) * (HEAD_DIM ** -0.5) for n, r0 in zip(ns, r0s)]
        k = [l2n(conv_silu(k_ref, cwk_ref, e, n, r0)) for n, r0 in zip(ns, r0s)]
        v = [conv_silu(v_ref, cwv_ref, e, n, r0) for n, r0 in zip(ns, r0s)]
        beta = [beta_s[e, pl.ds(r0, C), :] for r0 in r0s]
        gcb = [gc_s[e, pl.ds(r0, C), :] for r0 in r0s]
        decay = [jnp.exp(jnp.where(tril, g[:, :C] - row_ref[0, e, pl.ds(n, 1), :], NEG_BIG))
                 for n, g in zip(ns, gcb)]
        k_beta = [a * b for a, b in zip(k, beta)]
        qk = [_dot_nt(jnp.concatenate([a, b], axis=0).astype(BF16), c.astype(BF16))
              for a, b, c in zip(q, k_beta, k)]
        intra = [jnp.where(tril, a[:C] * d, 0.0).astype(BF16) for a, d in zip(qk, decay)]
        eg = [jnp.exp(g) for g in gcb]
        x = [jnp.concatenate([a * b, c * d], axis=1) for a, b, c, d in zip(v, beta, k_beta, eg)]
        m = [jnp.where(strict, -(a[C:] * d), 0.0) for a, d in zip(qk, decay)]
        n_steps = C.bit_length() - 1
        for j in range(n_steps):
            mb = [a.astype(BF16) for a in m]
            x = [a + _dot(b, a.astype(BF16)) for a, b in zip(x, mb)]
            if j + 1 < n_steps:
                m = [_dot(b, b) for b in mb]
        gl = [g[C - 1:C, :] for g in gcb]
        xb = [a.astype(BF16) for a in x]
        kd = [(a * jnp.exp(l - g)).astype(BF16) for a, l, g in zip(k, gl, gcb)]
        kuw = [_dot_tn(a, b) for a, b in zip(kd, xb)]
        auw = [_dot(a, b) for a, b in zip(intra, xb)]
        for i, n in enumerate(ns):
            p0 = pl.multiple_of(n * (HEAD_DIM + C), HEAD_DIM + C)
            pg_s[e, pl.ds(p0, HEAD_DIM), :] = kuw[i][:, HEAD_DIM:].astype(BF16)
            pg_s[e, pl.ds(p0 + HEAD_DIM, C), :] = (q[i] * eg[i] - auw[i][:, HEAD_DIM:]).astype(BF16)
            qn_s[e, pl.ds(pl.multiple_of(n * HEAD_DIM, HEAD_DIM), HEAD_DIM), :] = kuw[i][:, :HEAD_DIM]
            on_s[e, pl.ds(r0s[i], C), :] = auw[i][:, :HEAD_DIM]
            egl_s[e, pl.ds(pl.multiple_of(n * 8, 8), 8), :] = jnp.broadcast_to(jnp.exp(gl[i]), (8, HEAD_DIM))

    for e in range(hp):
        def body(it, carry, e=e):
            prep_body(e, it)
            return carry

        lax.fori_loop(0, n_chunks // chunks_per_iter, body, 0)

    gain = gain_ref[...]

    def emit(n, o_raw):
        r0 = pl.multiple_of(n * C, C)
        for e in range(hp):
            z = z_ref[0, e, pl.ds(r0, C), :]
            o_ref[0, pl.ds(r0, C), e * LANES:(e + 1) * LANES] = (
                _rms(o_raw[e], gain) * (z * _sigmoid(z))).astype(o_ref.dtype)

    def step(n, carry):
        states, o_prev = carry
        r0 = pl.multiple_of(n * C, C)
        p0 = pl.multiple_of(n * (HEAD_DIM + C), HEAD_DIM + C)
        n0 = pl.multiple_of(n * HEAD_DIM, HEAD_DIM)
        ps = [_dot(pg_s[e, pl.ds(p0, HEAD_DIM + C), :], states[e].astype(BF16))
              for e in range(hp)]
        emit(jnp.maximum(n - 1, 0), o_prev)
        new_states = []
        for e in range(hp):
            egl = egl_s[e, pl.ds(pl.multiple_of(n * 8, 8), 1), :]
            new_states.append(states[e] * jnp.broadcast_to(egl, states[e].shape)
                              + (qn_s[e, pl.ds(n0, HEAD_DIM), :] - ps[e][:HEAD_DIM]))
        o_raw = tuple(ps[e][HEAD_DIM:] + on_s[e, pl.ds(r0, C), :] for e in range(hp))
        return tuple(new_states), o_raw

    init = (tuple(jnp.zeros((HEAD_DIM, HEAD_DIM), F32) for _ in range(hp)),
            tuple(jnp.zeros((C, HEAD_DIM), F32) for _ in range(hp)))
    _, o_last = lax.fori_loop(0, n_chunks, step, init)
    emit(n_chunks - 1, o_last)


def _gdn(proj, conv_w, col, row, o_gain, *, n_heads, chunks_per_iter=32):
    B, _, S, _ = proj.shape
    H = n_heads
    hp = 2 if H % 2 == 0 else 1
    n_chunks = S // GDN_CHUNK

    def slab(off):
        return pl.BlockSpec((1, hp, S, LANES), lambda b, h: (b, off // hp + h, 0, 0))

    def cw(off):
        return pl.BlockSpec((GDN_CONV, hp * LANES), lambda b, h: (0, off // hp + h))

    def per_head(rows, dtype):
        return pltpu.VMEM((hp, rows, HEAD_DIM), dtype)

    return pl.pallas_call(
        functools.partial(_gdn_kernel, n_heads=H, heads_per_step=hp,
                          chunks_per_iter=_tile(n_chunks, chunks_per_iter)),
        grid=(B, H // hp),
        in_specs=[
            slab(0), slab(H), slab(2 * H), slab(3 * H),
            cw(0), cw(H), cw(2 * H),
            pl.BlockSpec((1, S, LANES), lambda b, h: (b, 0, 0)),
            pl.BlockSpec((1, hp, n_chunks, GDN_CHUNK), lambda b, h: (b, H // hp + h, 0, 0)),
            pl.BlockSpec((1, LANES), lambda b, h: (0, 0)),
        ],
        out_specs=pl.BlockSpec((1, S, hp * LANES), lambda b, h: (b, 0, h)),
        out_shape=jax.ShapeDtypeStruct((B, S, H * HEAD_DIM), BF16),
        scratch_shapes=[
            per_head(S, F32),
            per_head(S, F32),
            per_head(n_chunks * (HEAD_DIM + GDN_CHUNK), BF16),
            per_head(n_chunks * HEAD_DIM, F32),
            per_head(S, F32),
            per_head(n_chunks * 8, F32),
        ],
        compiler_params=_params(2),
        name="gdn",
    )(proj, proj, proj, proj, conv_w, conv_w, conv_w, col, row, o_gain)


def _sb_kernel(q_ref, k_ref, v_ref, qg_ref, kg_ref, og_ref, o_ref, qs, ks, vs, *, group):
    T = SB_BLOCK
    G = group
    Q = G * T
    S = q_ref.shape[2]
    qs[...] = _rms(q_ref[0, 0], qg_ref[...]).astype(BF16)
    ks[...] = _rms(k_ref[0, 0], kg_ref[...]).astype(BF16)
    vs[...] = v_ref[0, 0].astype(BF16)
    scale = HEAD_DIM ** -0.5 * LOG2E
    og = og_ref[...]
    sign_bit = jnp.int32(-2 ** 31)

    ri = lax.broadcasted_iota(jnp.int32, (T, T), 0)
    ci = lax.broadcasted_iota(jnp.int32, (T, T), 1)
    past = ci < ri
    jj = lax.broadcasted_iota(jnp.int32, (2 * T, 2 * T), 0) % T
    ss = lax.broadcasted_iota(jnp.int32, (2 * T, 2 * T), 1)
    uneg = jnp.where((ss >= T) | (jj >= ss), -1.0, 0.0).astype(BF16)

    def stage_a(qb, k0, diag):
        z = _dot_nt(qb, ks[k0:k0 + Q, :]) * scale
        zs, lhs = [], []
        for d in range(G):
            zd = z[d * T if diag else 0:, d * T:(d + 1) * T]
            neg_abs = lax.bitcast_convert_type(lax.bitcast_convert_type(zd, jnp.int32) | sign_bit, F32)
            sp = jnp.maximum(zd, 0.0) + jnp.log(1.0 + jnp.exp2(neg_abs)) * LOG2E
            if diag:
                top = jnp.where(past, sp[:T], 0.0)
                sp = top if d == G - 1 else jnp.concatenate([top, sp[T:]], axis=0)
            hi = sp.astype(BF16)
            lo = (sp - hi.astype(F32)).astype(BF16)
            zs.append(zd)
            lhs.append(jnp.concatenate([hi, lo], axis=1))
        comb = _dot(lhs[0] if G == 1 else jnp.concatenate(lhs, axis=0), uneg)
        return zs, comb

    def stage_b(zs, comb, carry, k0, diag):
        offs = [0]
        for d in range(G):
            offs.append(offs[-1] + zs[d].shape[0])
        w_blocks = [None] * G
        for d in range(G - 1, -1, -1):
            lo_r = d * T if diag else 0
            cd = comb[offs[d]:offs[d + 1]]
            c_rows = carry[lo_r:]
            logw = zs[d] + c_rows + cd[:, :T]
            if diag:
                top = jnp.where(past, logw[:T], NEG_BIG)
                logw = top if d == G - 1 else jnp.concatenate([top, logw[T:]], axis=0)
            w = jnp.exp2(logw).astype(BF16)
            if lo_r:
                w = jnp.concatenate([jnp.zeros((lo_r, T), BF16), w], axis=0)
            w_blocks[d] = w
            c_new = c_rows + cd[:, T:]
            carry = c_new if lo_r == 0 else jnp.concatenate([carry[:lo_r], c_new], axis=0)
        wt = w_blocks[0] if G == 1 else jnp.concatenate(w_blocks, axis=1)
        return carry, _dot(wt, vs[k0:k0 + Q, :])

    tiles = []
    for i in range(S // Q):
        tiles.append((i, i * Q, True))
        tiles.extend((i, (i - 1 - t) * Q, False) for t in range(i))
    nxt = stage_a(qs[0:Q, :], 0, True)
    carry = acc = None
    for n, (i, k0, diag) in enumerate(tiles):
        cur = nxt
        if n + 1 < len(tiles):
            i2, k2, diag2 = tiles[n + 1]
            nxt = stage_a(qs[i2 * Q:(i2 + 1) * Q, :], k2, diag2)
        if diag:
            carry, acc = jnp.zeros((Q, T), F32), None
        carry, contrib = stage_b(cur[0], cur[1], carry, k0, diag)
        acc = contrib if acc is None else acc + contrib
        if n + 1 == len(tiles) or tiles[n + 1][2]:
            o_ref[0, i * Q:(i + 1) * Q, :] = _rms(acc, og).astype(o_ref.dtype)


def _sb(proj, q_gain, k_gain, o_gain, *, n_heads, slab0):
    B, _, S, _ = proj.shape
    H = n_heads
    group = _tile(S // SB_BLOCK, 4)

    def slab(off):
        return pl.BlockSpec((1, 1, S, LANES), lambda b, h: (b, off + h, 0, 0))

    gain = pl.BlockSpec((1, LANES), lambda b, h: (0, 0))
    return pl.pallas_call(
        functools.partial(_sb_kernel, group=group),
        grid=(B, H),
        in_specs=[slab(slab0), slab(slab0 + H), slab(slab0 + 2 * H), gain, gain, gain],
        out_specs=pl.BlockSpec((1, S, LANES), lambda b, h: (b, 0, h)),
        out_shape=jax.ShapeDtypeStruct((B, S, H * HEAD_DIM), BF16),
        scratch_shapes=[pltpu.VMEM((S, HEAD_DIM), BF16)] * 3,
        compiler_params=_params(2),
        name="sb",
    )(proj, proj, proj, q_gain, k_gain, o_gain)


def _outproj_kernel(yg_ref, ys_ref, w_ref, x_ref, o_ref):
    kg = yg_ref.shape[2]
    o_ref[0] = x_ref[0] + _dot(yg_ref[0], w_ref[:kg, :]) + _dot(ys_ref[0], w_ref[kg:, :])


def _outproj(yg, ys, w, x, *, tm, tn):
    B, S, D = x.shape
    Kg, Ks = yg.shape[2], ys.shape[2]
    return pl.pallas_call(
        _outproj_kernel,
        grid=(B, S // tm, D // tn),
        in_specs=[
            pl.BlockSpec((1, tm, Kg), lambda b, i, j: (b, i, 0)),
            pl.BlockSpec((1, tm, Ks), lambda b, i, j: (b, i, 0)),
            pl.BlockSpec((Kg + Ks, tn), lambda b, i, j: (0, j)),
            pl.BlockSpec((1, tm, tn), lambda b, i, j: (b, i, j)),
        ],
        out_specs=pl.BlockSpec((1, tm, tn), lambda b, i, j: (b, i, j)),
        out_shape=jax.ShapeDtypeStruct((B, S, D), F32),
        compiler_params=_params(3),
        name="outproj",
    )(yg, ys, w, x)


def _ffn_up_kernel(x_ref, halo_ref, g_ref, wg_ref, wv_ref, cg_ref, cv_ref, bg_ref, bv_ref, o_ref,
                   h_scr, ug_scr, uv_scr):
    tm = x_ref.shape[1]

    @pl.when(pl.program_id(2) == 0)
    def _():
        gain = g_ref[...]
        hh = _rms(halo_ref[0], gain)
        hh = jnp.where(pl.program_id(1) > 0, hh, 0.0)
        h_scr[:HALO, :] = hh.astype(BF16)
        h_scr[HALO:, :] = _rms(x_ref[0], gain).astype(BF16)

    hb = h_scr[...]
    ug_scr[...] = _dot(hb, wg_ref[...])
    uv_scr[...] = _dot(hb, wv_ref[...])

    def conv(u_scr, c_ref, b_ref):
        cw = c_ref[...]
        y = b_ref[...] + cw[FFN_CONV - 1:FFN_CONV] * u_scr[pl.ds(HALO, tm), :]
        for i in range(FFN_CONV - 1):
            y = y + cw[i:i + 1] * u_scr[pl.ds(HALO - (FFN_CONV - 1) + i, tm), :]
        return y

    gate = conv(ug_scr, cg_ref, bg_ref)
    val = conv(uv_scr, cv_ref, bv_ref)
    o_ref[0] = (gate * _sigmoid(gate) * val).astype(o_ref.dtype)


def _ffn_up(x, gain, w_up, conv_w, bias, *, tm, tn):
    B, S, D = x.shape
    F = w_up.shape[1] // 2
    nf = F // tn
    hb = tm // HALO
    return pl.pallas_call(
        _ffn_up_kernel,
        grid=(B, S // tm, nf),
        in_specs=[
            pl.BlockSpec((1, tm, D), lambda b, i, j: (b, i, 0)),
            pl.BlockSpec((1, HALO, D), lambda b, i, j: (b, jnp.maximum(i * hb - 1, 0), 0)),
            pl.BlockSpec((1, D), lambda b, i, j: (0, 0)),
            pl.BlockSpec((D, tn), lambda b, i, j: (0, j)),
            pl.BlockSpec((D, tn), lambda b, i, j: (0, nf + j)),
            pl.BlockSpec((FFN_CONV, tn), lambda b, i, j: (0, j)),
            pl.BlockSpec((FFN_CONV, tn), lambda b, i, j: (0, nf + j)),
            pl.BlockSpec((1, tn), lambda b, i, j: (0, j)),
            pl.BlockSpec((1, tn), lambda b, i, j: (0, nf + j)),
        ],
        out_specs=pl.BlockSpec((1, tm, tn), lambda b, i, j: (b, i, j)),
        out_shape=jax.ShapeDtypeStruct((B, S, F), BF16),
        scratch_shapes=[
            pltpu.VMEM((tm + HALO, D), BF16),
            pltpu.VMEM((tm + HALO, tn), F32),
            pltpu.VMEM((tm + HALO, tn), F32),
        ],
        compiler_params=_params(3),
        name="ffn_up",
    )(x, x, gain, w_up, w_up, conv_w, conv_w, bias, bias)


def _ffn_down_kernel(a_ref, w_ref, x_ref, o_ref):
    o_ref[0] = x_ref[0] + _dot(a_ref[0], w_ref[...])


def _ffn_down(act, w, x, *, tm, tn):
    B, S, D = x.shape
    F = act.shape[2]
    return pl.pallas_call(
        _ffn_down_kernel,
        grid=(B, S // tm, D // tn),
        in_specs=[
            pl.BlockSpec((1, tm, F), lambda b, i, j: (b, i, 0)),
            pl.BlockSpec((F, tn), lambda b, i, j: (0, j)),
            pl.BlockSpec((1, tm, tn), lambda b, i, j: (b, i, j)),
        ],
        out_specs=pl.BlockSpec((1, tm, tn), lambda b, i, j: (b, i, j)),
        out_shape=jax.ShapeDtypeStruct((B, S, D), F32),
        compiler_params=_params(3),
        name="ffn_down",
    )(act, w, x)


def _tile(n, pref):
    t = min(n, pref)
    while n % t:
        t //= 2
    return t


def kernel(x, attn_norm, w_in, gdn_conv, gdn_a_log, gdn_dt_bias, gdn_o_norm, sb_q_norm, sb_k_norm, sb_o_norm,
           w_out, ffn_norm, w_up, ffn_conv, ffn_conv_bias, w_down):
    B, S, D = x.shape
    depth = w_in.shape[0]
    H = gdn_a_log.shape[1]
    W = H * HEAD_DIM
    assert w_in.shape[2] == 7 * W + 2 * H and S % LANES == 0 and 2 * H <= LANES
    F = w_down.shape[1]
    tm = _tile(S, 1024)
    tn_in = _tile(7 * W, 1024)
    tn_d = _tile(D, 512)
    tn_o = _tile(D, 1024)
    tn_f = _tile(F, 512)

    def lane_row(v, off):
        return jnp.zeros((1, LANES), F32).at[0, off:off + v.shape[0]].set(v)

    for l in range(depth):
        wl = w_in[l]
        w_main = jnp.concatenate([wl[:, :4 * W], wl[:, 4 * W + 2 * H:]], axis=1).astype(BF16)
        w_ba = jnp.pad(wl[:, 4 * W:4 * W + 2 * H], ((0, 0), (0, LANES - 2 * H))).astype(BF16)
        proj, ba = _inproj(x, attn_norm[l][None], w_main, w_ba, tm=tm, tn=tn_in)
        col, row = _gdn_prep(ba, lane_row(gdn_a_log[l], H), lane_row(gdn_dt_bias[l], H), n_heads=H)
        row = row.reshape(B, LANES, S // GDN_CHUNK, GDN_CHUNK)
        y_gdn = _gdn(proj, gdn_conv[l], col, row, gdn_o_norm[l][None], n_heads=H)
        y_sb = _sb(proj, sb_q_norm[l][None], sb_k_norm[l][None], sb_o_norm[l][None], n_heads=H, slab0=4 * H)
        x = _outproj(y_gdn, y_sb, w_out[l].astype(BF16), x, tm=tm, tn=tn_o)
        act = _ffn_up(x, ffn_norm[l][None], w_up[l].astype(BF16), ffn_conv[l], ffn_conv_bias[l][None],
                      tm=tm, tn=tn_f)
        x = _ffn_down(act, w_down[l].astype(BF16), x, tm=tm, tn=tn_d)
    return x
```

```python
import functools

import jax
import jax.numpy as jnp
from jax import lax
from jax.experimental import pallas as pl
from jax.experimental.pallas import tpu as pltpu

F32 = jnp.float32
BF16 = jnp.bfloat16
RMS_EPS = 1e-6
L2_EPS = 1e-6
HEAD_DIM = 128
LANES = 128
GDN_CHUNK = 64
GDN_CONV = 4
FFN_CONV = 3
SB_BLOCK = 128
HALO = 16
NEG_BIG = -1e30
LOG2E = 1.4426950408889634
VMEM_LIMIT = 56 * 1024 * 1024


def _dot(a, b):
    return jnp.dot(a, b, preferred_element_type=F32)


def _dot_nt(a, b):
    return lax.dot_general(a, b, (((1,), (1,)), ((), ())), preferred_element_type=F32)


def _dot_tn(a, b):
    return lax.dot_general(a, b, (((0,), (0,)), ((), ())), preferred_element_type=F32)


def _sigmoid(x):
    return 1.0 / (1.0 + jnp.exp(-x))


def _softplus(x):
    return jnp.maximum(x, 0.0) + jnp.log1p(jnp.exp(-jnp.abs(x)))


def _rms(x, gain):
    return (x * lax.rsqrt(jnp.mean(x * x, axis=-1, keepdims=True) + RMS_EPS)) * gain


def _params(n_axes):
    return pltpu.CompilerParams(dimension_semantics=("arbitrary",) * n_axes, vmem_limit_bytes=VMEM_LIMIT)


def _inproj_kernel(x_ref, g_ref, wa_ref, wb_ref, wba_ref, o_ref, ba_ref, h_scr, *, n_slabs, n_a):
    j = pl.program_id(2)

    @pl.when(j == 0)
    def _():
        hb = _rms(x_ref[0], g_ref[...]).astype(BF16)
        h_scr[...] = hb
        ba_ref[0] = _dot(hb, wba_ref[...])

    def emit(w_ref):
        res = _dot(h_scr[...], w_ref[...])
        for c in range(n_slabs):
            o_ref[0, c] = res[:, c * LANES:(c + 1) * LANES]

    pl.when(j < n_a)(lambda: emit(wa_ref))
    pl.when(j >= n_a)(lambda: emit(wb_ref))


def _inproj(x, gain, w_a, w_b, w_ba, layer, *, tm, tn):
    B, S, D = x.shape
    n_a, n_b = w_a.shape[2] // tn, w_b.shape[2] // tn
    n_slabs = tn // LANES
    return pl.pallas_call(
        functools.partial(_inproj_kernel, n_slabs=n_slabs, n_a=n_a),
        grid=(B, S // tm, n_a + n_b),
        in_specs=[
            pl.BlockSpec((1, tm, D), lambda b, i, j: (b, i, 0)),
            pl.BlockSpec((1, D), lambda b, i, j: (0, 0)),
            pl.BlockSpec((None, D, tn), lambda b, i, j: (layer, 0, jnp.minimum(j, n_a - 1))),
            pl.BlockSpec((None, D, tn), lambda b, i, j: (layer, 0, jnp.maximum(j - n_a, 0))),
            pl.BlockSpec((None, D, LANES), lambda b, i, j: (layer, 0, 0)),
        ],
        out_specs=[
            pl.BlockSpec((1, n_slabs, tm, LANES), lambda b, i, j: (b, j, i, 0)),
            pl.BlockSpec((1, tm, LANES), lambda b, i, j: (b, i, 0)),
        ],
        out_shape=[
            jax.ShapeDtypeStruct((B, (n_a + n_b) * n_slabs, S, LANES), F32),
            jax.ShapeDtypeStruct((B, S, LANES), F32),
        ],
        scratch_shapes=[pltpu.VMEM((tm, D), BF16)],
        compiler_params=_params(3),
        name="inproj",
    )(x, gain, w_a, w_b, w_ba)


def _gdn_prep_kernel(ba_ref, alog_ref, dtb_ref, col_ref, row_ref, *, n_heads):
    ba = ba_ref[0]
    S = ba.shape[0]
    g = -jnp.exp(alog_ref[...]) * _softplus(ba + dtb_ref[...])
    r = lax.broadcasted_iota(jnp.int32, (LANES, LANES), 0)
    c = lax.broadcasted_iota(jnp.int32, (LANES, LANES), 1)
    tril = jnp.where((r // GDN_CHUNK == c // GDN_CHUNK) & (c <= r), 1.0, 0.0).astype(F32)
    parts = [
        jnp.dot(tril, g[i * LANES:(i + 1) * LANES], precision=lax.Precision.HIGHEST, preferred_element_type=F32)
        for i in range(S // LANES)
    ]
    gc = jnp.concatenate(parts, axis=0)
    lane = lax.broadcasted_iota(jnp.int32, ba.shape, 1)
    col = jnp.where(lane < n_heads, _sigmoid(ba), gc)
    col_ref[0] = col
    row_ref[0] = col.T


def _gdn_prep(ba, alog_l, dtb_l, *, n_heads):
    B, S, _ = ba.shape
    return pl.pallas_call(
        functools.partial(_gdn_prep_kernel, n_heads=n_heads),
        grid=(B,),
        in_specs=[
            pl.BlockSpec((1, S, LANES), lambda b: (b, 0, 0)),
            pl.BlockSpec((1, LANES), lambda b: (0, 0)),
            pl.BlockSpec((1, LANES), lambda b: (0, 0)),
        ],
        out_specs=[
            pl.BlockSpec((1, S, LANES), lambda b: (b, 0, 0)),
            pl.BlockSpec((1, LANES, S), lambda b: (b, 0, 0)),
        ],
        out_shape=[
            jax.ShapeDtypeStruct((B, S, LANES), F32),
            jax.ShapeDtypeStruct((B, LANES, S), F32),
        ],
        compiler_params=_params(1),
        name="gdn_prep",
    )(ba, alog_l, dtb_l)


def _gdn_kernel(q_ref, k_ref, v_ref, z_ref, cwq_ref, cwk_ref, cwv_ref, col_ref, row_ref, gain_ref, o_ref,
                beta_s, gc_s, pg_s, qn_s, on_s, egl_s, xpad_s, *, n_heads, heads_per_step, chunks_per_iter):
    C = GDN_CHUNK
    PAD = 8
    S = col_ref.shape[1]
    n_chunks = S // C
    hp = heads_per_step
    h0 = pl.program_id(1) * hp

    colv = col_ref[0]
    lane = lax.broadcasted_iota(jnp.int32, colv.shape, 1)
    for e in range(hp):
        beta_s[e] = jnp.broadcast_to(
            jnp.sum(jnp.where(lane == h0 + e, colv, 0.0), axis=1, keepdims=True), colv.shape)
        gc_s[e] = jnp.broadcast_to(
            jnp.sum(jnp.where(lane == h0 + e + n_heads, colv, 0.0), axis=1, keepdims=True), colv.shape)

    ri = lax.broadcasted_iota(jnp.int32, (C, C), 0)
    ci = lax.broadcasted_iota(jnp.int32, (C, C), 1)
    tril = ci <= ri
    strict = ci < ri

    for e in range(hp):
        for t, x_ref in enumerate((q_ref, k_ref, v_ref)):
            xpad_s[e, t, 0:PAD, :] = jnp.zeros((PAD, LANES), F32)
            xpad_s[e, t, PAD:, :] = x_ref[0, e]

    def conv_silu(t, cw_ref, e, r0):
        cw = cw_ref[:, e * LANES:(e + 1) * LANES]
        y = cw[GDN_CONV - 1:GDN_CONV] * xpad_s[e, t, pl.ds(r0 + PAD, C), :]
        for i in range(GDN_CONV - 1):
            y = y + cw[i:i + 1] * xpad_s[e, t, pl.ds(r0 + PAD - (GDN_CONV - 1) + i, C), :]
        return y * _sigmoid(y)

    def l2n(x):
        return x * lax.rsqrt(jnp.sum(x * x, axis=-1, keepdims=True) + L2_EPS)

    def prep_body(e, it):
        ns = [it * chunks_per_iter + s for s in range(chunks_per_iter)]
        r0s = [pl.multiple_of(n * C, C) for n in ns]
        q = [l2n(conv_silu(0, cwq_ref, e, r0)) * (HEAD_DIM ** -0.5) for r0 in r0s]
        k = [l2n(conv_silu(1, cwk_ref, e, r0)) for r0 in r0s]
        v = [conv_silu(2, cwv_ref, e, r0) for r0 in r0s]
        beta = [beta_s[e, pl.ds(r0, C), :] for r0 in r0s]
        gcb = [gc_s[e, pl.ds(r0, C), :] for r0 in r0s]
        decay = [jnp.exp(jnp.where(tril, g[:, :C] - row_ref[0, e, pl.ds(n, 1), :], NEG_BIG))
                 for n, g in zip(ns, gcb)]
        k_beta = [a * b for a, b in zip(k, beta)]
        qk = [_dot_nt(jnp.concatenate([a, b], axis=0).astype(BF16), c.astype(BF16))
              for a, b, c in zip(q, k_beta, k)]
        intra = [jnp.where(tril, a[:C] * d, 0.0).astype(BF16) for a, d in zip(qk, decay)]
        eg = [jnp.exp(g) for g in gcb]
        x = [jnp.concatenate([a * b, c * d], axis=1) for a, b, c, d in zip(v, beta, k_beta, eg)]
        m = [jnp.where(strict, -(a[C:] * d), 0.0) for a, d in zip(qk, decay)]
        n_steps = C.bit_length() - 1
        for j in range(n_steps):
            mb = [a.astype(BF16) for a in m]
            x = [a + _dot(b, a.astype(BF16)) for a, b in zip(x, mb)]
            if j + 1 < n_steps:
                m = [_dot(b, b) for b in mb]
        gl = [g[C - 1:C, :] for g in gcb]
        xb = [a.astype(BF16) for a in x]
        kd = [(a * jnp.exp(l - g)).astype(BF16) for a, l, g in zip(k, gl, gcb)]
        kuw = [_dot_tn(a, b) for a, b in zip(kd, xb)]
        auw = [_dot(a, b) for a, b in zip(intra, xb)]
        for i, n in enumerate(ns):
            p0 = pl.multiple_of(n * (HEAD_DIM + C), HEAD_DIM + C)
            pg_s[e, pl.ds(p0, HEAD_DIM), :] = kuw[i][:, HEAD_DIM:].astype(BF16)
            pg_s[e, pl.ds(p0 + HEAD_DIM, C), :] = (q[i] * eg[i] - auw[i][:, HEAD_DIM:]).astype(BF16)
            qn_s[e, pl.ds(pl.multiple_of(n * HEAD_DIM, HEAD_DIM), HEAD_DIM), :] = kuw[i][:, :HEAD_DIM]
            on_s[e, pl.ds(r0s[i], C), :] = auw[i][:, :HEAD_DIM]
            egl_s[e, pl.ds(pl.multiple_of(n * 8, 8), 8), :] = jnp.broadcast_to(jnp.exp(gl[i]), (8, HEAD_DIM))

    for e in range(hp):
        def body(it, carry, e=e):
            prep_body(e, it)
            return carry

        lax.fori_loop(0, n_chunks // chunks_per_iter, body, 0)

    gain = gain_ref[...]

    def emit(n, o_raw):
        r0 = pl.multiple_of(n * C, C)
        for e in range(hp):
            z = z_ref[0, e, pl.ds(r0, C), :]
            o_ref[0, pl.ds(r0, C), e * LANES:(e + 1) * LANES] = (
                _rms(o_raw[e], gain) * (z * _sigmoid(z))).astype(o_ref.dtype)

    def step(n, carry):
        states, o_prev = carry
        r0 = pl.multiple_of(n * C, C)
        p0 = pl.multiple_of(n * (HEAD_DIM + C), HEAD_DIM + C)
        n0 = pl.multiple_of(n * HEAD_DIM, HEAD_DIM)
        ps = [_dot(pg_s[e, pl.ds(p0, HEAD_DIM + C), :], states[e].astype(BF16))
              for e in range(hp)]
        emit(jnp.maximum(n - 1, 0), o_prev)
        new_states = []
        for e in range(hp):
            egl = egl_s[e, pl.ds(pl.multiple_of(n * 8, 8), 1), :]
            new_states.append(states[e] * jnp.broadcast_to(egl, states[e].shape)
                              + (qn_s[e, pl.ds(n0, HEAD_DIM), :] - ps[e][:HEAD_DIM]))
        o_raw = tuple(ps[e][HEAD_DIM:] + on_s[e, pl.ds(r0, C), :] for e in range(hp))
        return tuple(new_states), o_raw

    init = (tuple(jnp.zeros((HEAD_DIM, HEAD_DIM), F32) for _ in range(hp)),
            tuple(jnp.zeros((C, HEAD_DIM), F32) for _ in range(hp)))
    _, o_last = lax.fori_loop(0, n_chunks, step, init)
    emit(n_chunks - 1, o_last)


def _gdn(proj, conv_w, col, row, o_gain, *, n_heads, chunks_per_iter=32):
    B, _, S, _ = proj.shape
    H = n_heads
    hp = 2 if H % 2 == 0 else 1
    n_chunks = S // GDN_CHUNK

    def slab(off):
        return pl.BlockSpec((1, hp, S, LANES), lambda b, h: (b, off // hp + h, 0, 0))

    def cw(off):
        return pl.BlockSpec((GDN_CONV, hp * LANES), lambda b, h: (0, off // hp + h))

    def per_head(rows, dtype):
        return pltpu.VMEM((hp, rows, HEAD_DIM), dtype)

    return pl.pallas_call(
        functools.partial(_gdn_kernel, n_heads=H, heads_per_step=hp,
                          chunks_per_iter=_tile(n_chunks, chunks_per_iter)),
        grid=(B, H // hp),
        in_specs=[
            slab(0), slab(H), slab(2 * H), slab(3 * H),
            cw(0), cw(H), cw(2 * H),
            pl.BlockSpec((1, S, LANES), lambda b, h: (b, 0, 0)),
            pl.BlockSpec((1, hp, n_chunks, GDN_CHUNK), lambda b, h: (b, H // hp + h, 0, 0)),
            pl.BlockSpec((1, LANES), lambda b, h: (0, 0)),
        ],
        out_specs=pl.BlockSpec((1, S, hp * LANES), lambda b, h: (b, 0, h)),
        out_shape=jax.ShapeDtypeStruct((B, S, H * HEAD_DIM), BF16),
        scratch_shapes=[
            per_head(S, F32),
            per_head(S, F32),
            per_head(n_chunks * (HEAD_DIM + GDN_CHUNK), BF16),
            per_head(n_chunks * HEAD_DIM, F32),
            per_head(S, F32),
            per_head(n_chunks * 8, F32),
            pltpu.VMEM((hp, 3, S + 8, LANES), F32),
        ],
        compiler_params=_params(2),
        name="gdn",
    )(proj, proj, proj, proj, conv_w, conv_w, conv_w, col, row, o_gain)


def _sb_kernel(q_ref, k_ref, v_ref, qg_ref, kg_ref, og_ref, o_ref, qs, ks, vs, *, group):
    T = SB_BLOCK
    G = group
    Q = G * T
    S = q_ref.shape[2]
    qs[...] = _rms(q_ref[0, 0], qg_ref[...]).astype(BF16)
    ks[...] = _rms(k_ref[0, 0], kg_ref[...]).astype(BF16)
    vs[...] = v_ref[0, 0].astype(BF16)
    scale = HEAD_DIM ** -0.5 * LOG2E
    og = og_ref[...]
    sign_bit = jnp.int32(-2 ** 31)

    ri = lax.broadcasted_iota(jnp.int32, (T, T), 0)
    ci = lax.broadcasted_iota(jnp.int32, (T, T), 1)
    past = ci < ri
    jj = lax.broadcasted_iota(jnp.int32, (2 * T, 2 * T), 0) % T
    ss = lax.broadcasted_iota(jnp.int32, (2 * T, 2 * T), 1)
    uneg = jnp.where((ss >= T) | (jj >= ss), -1.0, 0.0).astype(BF16)

    def stage_a(qb, k0, diag):
        z = _dot_nt(qb, ks[k0:k0 + Q, :]) * scale
        zs, lhs = [], []
        for d in range(G):
            zd = z[d * T if diag else 0:, d * T:(d + 1) * T]
            neg_abs = lax.bitcast_convert_type(lax.bitcast_convert_type(zd, jnp.int32) | sign_bit, F32)
            sp = jnp.maximum(zd, 0.0) + jnp.log(1.0 + jnp.exp2(neg_abs)) * LOG2E
            if diag:
                top = jnp.where(past, sp[:T], 0.0)
                sp = top if d == G - 1 else jnp.concatenate([top, sp[T:]], axis=0)
            hi = sp.astype(BF16)
            lo = (sp - hi.astype(F32)).astype(BF16)
            zs.append(zd)
            lhs.append(jnp.concatenate([hi, lo], axis=1))
        comb = _dot(lhs[0] if G == 1 else jnp.concatenate(lhs, axis=0), uneg)
        return zs, comb

    def stage_b(zs, comb, carry, k0, diag):
        offs = [0]
        for d in range(G):
            offs.append(offs[-1] + zs[d].shape[0])
        w_blocks = [None] * G
        for d in range(G - 1, -1, -1):
            lo_r = d * T if diag else 0
            cd = comb[offs[d]:offs[d + 1]]
            c_rows = carry[lo_r:]
            logw = zs[d] + c_rows + cd[:, :T]
            if diag:
                top = jnp.where(past, logw[:T], NEG_BIG)
                logw = top if d == G - 1 else jnp.concatenate([top, logw[T:]], axis=0)
            w = jnp.exp2(logw).astype(BF16)
            if lo_r:
                w = jnp.concatenate([jnp.zeros((lo_r, T), BF16), w], axis=0)
            w_blocks[d] = w
            c_new = c_rows + cd[:, T:]
            carry = c_new if lo_r == 0 else jnp.concatenate([carry[:lo_r], c_new], axis=0)
        wt = w_blocks[0] if G == 1 else jnp.concatenate(w_blocks, axis=1)
        return carry, _dot(wt, vs[k0:k0 + Q, :])

    tiles = []
    for i in range(S // Q):
        tiles.append((i, i * Q, True))
        tiles.extend((i, (i - 1 - t) * Q, False) for t in range(i))
    nxt = stage_a(qs[0:Q, :], 0, True)
    carry = acc = None
    for n, (i, k0, diag) in enumerate(tiles):
        cur = nxt
        if n + 1 < len(tiles):
            i2, k2, diag2 = tiles[n + 1]
            nxt = stage_a(qs[i2 * Q:(i2 + 1) * Q, :], k2, diag2)
        if diag:
            carry, acc = jnp.zeros((Q, T), F32), None
        carry, contrib = stage_b(cur[0], cur[1], carry, k0, diag)
        acc = contrib if acc is None else acc + contrib
        if n + 1 == len(tiles) or tiles[n + 1][2]:
            o_ref[0, i * Q:(i + 1) * Q, :] = _rms(acc, og).astype(o_ref.dtype)


def _sb(proj, q_gain, k_gain, o_gain, *, n_heads, slab0):
    B, _, S, _ = proj.shape
    H = n_heads
    group = _tile(S // SB_BLOCK, 4)

    def slab(off):
        return pl.BlockSpec((1, 1, S, LANES), lambda b, h: (b, off + h, 0, 0))

    gain = pl.BlockSpec((1, LANES), lambda b, h: (0, 0))
    return pl.pallas_call(
        functools.partial(_sb_kernel, group=group),
        grid=(B, H),
        in_specs=[slab(slab0), slab(slab0 + H), slab(slab0 + 2 * H), gain, gain, gain],
        out_specs=pl.BlockSpec((1, S, LANES), lambda b, h: (b, 0, h)),
        out_shape=jax.ShapeDtypeStruct((B, S, H * HEAD_DIM), BF16),
        scratch_shapes=[pltpu.VMEM((S, HEAD_DIM), BF16)] * 3,
        compiler_params=_params(2),
        name="sb",
    )(proj, proj, proj, q_gain, k_gain, o_gain)


def _outproj_kernel(yg_ref, ys_ref, w_ref, x_ref, o_ref):
    kg = yg_ref.shape[2]
    o_ref[0] = x_ref[0] + _dot(yg_ref[0], w_ref[:kg, :]) + _dot(ys_ref[0], w_ref[kg:, :])


def _outproj(yg, ys, w, layer, x, *, tm, tn):
    B, S, D = x.shape
    Kg, Ks = yg.shape[2], ys.shape[2]
    return pl.pallas_call(
        _outproj_kernel,
        grid=(B, S // tm, D // tn),
        in_specs=[
            pl.BlockSpec((1, tm, Kg), lambda b, i, j: (b, i, 0)),
            pl.BlockSpec((1, tm, Ks), lambda b, i, j: (b, i, 0)),
            pl.BlockSpec((None, Kg + Ks, tn), lambda b, i, j: (layer, 0, j)),
            pl.BlockSpec((1, tm, tn), lambda b, i, j: (b, i, j)),
        ],
        out_specs=pl.BlockSpec((1, tm, tn), lambda b, i, j: (b, i, j)),
        out_shape=jax.ShapeDtypeStruct((B, S, D), F32),
        compiler_params=_params(3),
        name="outproj",
    )(yg, ys, w, x)


def _ffn_up_kernel(x_ref, halo_ref, g_ref, wg_ref, wv_ref, cg_ref, cv_ref, bg_ref, bv_ref, o_ref,
                   h_scr, ug_scr, uv_scr):
    tm = x_ref.shape[1]

    @pl.when(pl.program_id(2) == 0)
    def _():
        gain = g_ref[...]
        hh = _rms(halo_ref[0], gain)
        hh = jnp.where(pl.program_id(1) > 0, hh, 0.0)
        h_scr[:HALO, :] = hh.astype(BF16)
        h_scr[HALO:, :] = _rms(x_ref[0], gain).astype(BF16)

    hb = h_scr[...]
    ug_scr[...] = _dot(hb, wg_ref[...])
    uv_scr[...] = _dot(hb, wv_ref[...])

    def conv(u_scr, c_ref, b_ref):
        cw = c_ref[...]
        y = b_ref[...] + cw[FFN_CONV - 1:FFN_CONV] * u_scr[pl.ds(HALO, tm), :]
        for i in range(FFN_CONV - 1):
            y = y + cw[i:i + 1] * u_scr[pl.ds(HALO - (FFN_CONV - 1) + i, tm), :]
        return y

    gate = conv(ug_scr, cg_ref, bg_ref)
    val = conv(uv_scr, cv_ref, bv_ref)
    o_ref[0] = (gate * _sigmoid(gate) * val).astype(o_ref.dtype)


def _ffn_up(x, gain, w_up, layer, conv_w, bias, *, tm, tn):
    B, S, D = x.shape
    F = w_up.shape[2] // 2
    nf = F // tn
    hb = tm // HALO
    return pl.pallas_call(
        _ffn_up_kernel,
        grid=(B, S // tm, nf),
        in_specs=[
            pl.BlockSpec((1, tm, D), lambda b, i, j: (b, i, 0)),
            pl.BlockSpec((1, HALO, D), lambda b, i, j: (b, jnp.maximum(i * hb - 1, 0), 0)),
            pl.BlockSpec((1, D), lambda b, i, j: (0, 0)),
            pl.BlockSpec((None, D, tn), lambda b, i, j: (layer, 0, j)),
            pl.BlockSpec((None, D, tn), lambda b, i, j: (layer, 0, nf + j)),
            pl.BlockSpec((FFN_CONV, tn), lambda b, i, j: (0, j)),
            pl.BlockSpec((FFN_CONV, tn), lambda b, i, j: (0, nf + j)),
            pl.BlockSpec((1, tn), lambda b, i, j: (0, j)),
            pl.BlockSpec((1, tn), lambda b, i, j: (0, nf + j)),
        ],
        out_specs=pl.BlockSpec((1, tm, tn), lambda b, i, j: (b, i, j)),
        out_shape=jax.ShapeDtypeStruct((B, S, F), BF16),
        scratch_shapes=[
            pltpu.VMEM((tm + HALO, D), BF16),
            pltpu.VMEM((tm + HALO, tn), F32),
            pltpu.VMEM((tm + HALO, tn), F32),
        ],
        compiler_params=_params(3),
        name="ffn_up",
    )(x, x, gain, w_up, w_up, conv_w, conv_w, bias, bias)


def _ffn_down_kernel(a_ref, w_ref, x_ref, o_ref):
    o_ref[0] = x_ref[0] + _dot(a_ref[0], w_ref[...])


def _ffn_down(act, w, layer, x, *, tm, tn):
    B, S, D = x.shape
    F = act.shape[2]
    return pl.pallas_call(
        _ffn_down_kernel,
        grid=(B, S // tm, D // tn),
        in_specs=[
            pl.BlockSpec((1, tm, F), lambda b, i, j: (b, i, 0)),
            pl.BlockSpec((None, F, tn), lambda b, i, j: (layer, 0, j)),
            pl.BlockSpec((1, tm, tn), lambda b, i, j: (b, i, j)),
        ],
        out_specs=pl.BlockSpec((1, tm, tn), lambda b, i, j: (b, i, j)),
        out_shape=jax.ShapeDtypeStruct((B, S, D), F32),
        compiler_params=_params(3),
        name="ffn_down",
    )(act, w, x)


def _tile(n, pref):
    t = min(n, pref)
    while n % t:
        t //= 2
    return t


def kernel(x, attn_norm, w_in, gdn_conv, gdn_a_log, gdn_dt_bias, gdn_o_norm, sb_q_norm, sb_k_norm, sb_o_norm,
           w_out, ffn_norm, w_up, ffn_conv, ffn_conv_bias, w_down):
    B, S, D = x.shape
    depth = w_in.shape[0]
    H = gdn_a_log.shape[1]
    W = H * HEAD_DIM
    assert w_in.shape[2] == 7 * W + 2 * H and S % LANES == 0 and 2 * H <= LANES
    F = w_down.shape[1]
    tm = _tile(S, 1024)
    tn_in = _tile(W, 1024)
    tn_d = _tile(D, 512)
    tn_o = _tile(D, 1024)
    tn_f = _tile(F, 512)

    def lane_row(v, off):
        return jnp.zeros((1, LANES), F32).at[0, off:off + v.shape[0]].set(v)

    w_gdn = w_in[..., :4 * W].astype(BF16)
    w_sb = w_in[..., 4 * W + 2 * H:].astype(BF16)
    w_ba = jnp.pad(w_in[..., 4 * W:4 * W + 2 * H], ((0, 0), (0, 0), (0, LANES - 2 * H))).astype(BF16)
    w_out_b, w_up_b, w_down_b = w_out.astype(BF16), w_up.astype(BF16), w_down.astype(BF16)
    for l in range(depth):
        proj, ba = _inproj(x, attn_norm[l][None], w_gdn, w_sb, w_ba, l, tm=tm, tn=tn_in)
        col, row = _gdn_prep(ba, lane_row(gdn_a_log[l], H), lane_row(gdn_dt_bias[l], H), n_heads=H)
        row = row.reshape(B, LANES, S // GDN_CHUNK, GDN_CHUNK)
        y_gdn = _gdn(proj, gdn_conv[l], col, row, gdn_o_norm[l][None], n_heads=H)
        y_sb = _sb(proj, sb_q_norm[l][None], sb_k_norm[l][None], sb_o_norm[l][None], n_heads=H, slab0=4 * H)
        x = _outproj(y_gdn, y_sb, w_out_b, l, x, tm=tm, tn=tn_o)
        act = _ffn_up(x, ffn_norm[l][None], w_up_b, l, ffn_conv[l], ffn_conv_bias[l][None], tm=tm, tn=tn_f)
        x = _ffn_down(act, w_down_b, l, x, tm=tm, tn=tn_d)
    return x
```

```python
import functools

import jax
import jax.numpy as jnp
from jax import lax
from jax.experimental import pallas as pl
from jax.experimental.pallas import tpu as pltpu

F32 = jnp.float32
BF16 = jnp.bfloat16
RMS_EPS = 1e-6
L2_EPS = 1e-6
HEAD_DIM = 128
LANES = 128
GDN_CHUNK = 64
GDN_CONV = 4
FFN_CONV = 3
SB_BLOCK = 128
HALO = 16
NEG_BIG = -1e30
LOG2E = 1.4426950408889634
VMEM_LIMIT = 56 * 1024 * 1024


def _dot(a, b):
    return jnp.dot(a, b, preferred_element_type=F32)


def _dot_nt(a, b):
    return lax.dot_general(a, b, (((1,), (1,)), ((), ())), preferred_element_type=F32)


def _dot_tn(a, b):
    return lax.dot_general(a, b, (((0,), (0,)), ((), ())), preferred_element_type=F32)


def _sigmoid(x):
    return 1.0 / (1.0 + jnp.exp(-x))


def _softplus(x):
    return jnp.maximum(x, 0.0) + jnp.log1p(jnp.exp(-jnp.abs(x)))


def _rms(x, gain):
    return (x * lax.rsqrt(jnp.mean(x * x, axis=-1, keepdims=True) + RMS_EPS)) * gain


def _params(n_axes):
    return pltpu.CompilerParams(dimension_semantics=("arbitrary",) * n_axes, vmem_limit_bytes=VMEM_LIMIT)


def _inproj_kernel(x_ref, g_ref, wa_ref, wb_ref, wba_ref, o_ref, ba_ref, h_scr, *, n_slabs, n_a):
    j = pl.program_id(2)

    @pl.when(j == 0)
    def _():
        hb = _rms(x_ref[0], g_ref[...]).astype(BF16)
        h_scr[...] = hb
        ba_ref[0] = _dot(hb, wba_ref[...])

    def emit(w_ref):
        res = _dot(h_scr[...], w_ref[...])
        for c in range(n_slabs):
            o_ref[0, c] = res[:, c * LANES:(c + 1) * LANES]

    pl.when(j < n_a)(lambda: emit(wa_ref))
    pl.when(j >= n_a)(lambda: emit(wb_ref))


def _inproj(x, gain, w_a, n_a, w_b, w_ba, layer, *, tm, tn):
    B, S, D = x.shape
    n_b = w_b.shape[2] // tn
    n_slabs = tn // LANES
    return pl.pallas_call(
        functools.partial(_inproj_kernel, n_slabs=n_slabs, n_a=n_a),
        grid=(B, S // tm, n_a + n_b),
        in_specs=[
            pl.BlockSpec((1, tm, D), lambda b, i, j: (b, i, 0)),
            pl.BlockSpec((1, D), lambda b, i, j: (0, 0)),
            pl.BlockSpec((None, D, tn), lambda b, i, j: (layer, 0, jnp.minimum(j, n_a - 1))),
            pl.BlockSpec((None, D, tn), lambda b, i, j: (layer, 0, jnp.where(j < n_a, n_b - 1, j - n_a))),
            pl.BlockSpec((None, D, LANES), lambda b, i, j: (layer, 0, 0)),
        ],
        out_specs=[
            pl.BlockSpec((1, n_slabs, tm, LANES), lambda b, i, j: (b, j, i, 0)),
            pl.BlockSpec((1, tm, LANES), lambda b, i, j: (b, i, 0)),
        ],
        out_shape=[
            jax.ShapeDtypeStruct((B, (n_a + n_b) * n_slabs, S, LANES), F32),
            jax.ShapeDtypeStruct((B, S, LANES), F32),
        ],
        scratch_shapes=[pltpu.VMEM((tm, D), BF16)],
        compiler_params=_params(3),
        name="inproj",
    )(x, gain, w_a, w_b, w_ba)


def _gdn_prep_kernel(ba_ref, alog_ref, dtb_ref, col_ref, row_ref, *, n_heads):
    ba = ba_ref[0]
    S = ba.shape[0]
    g = -jnp.exp(alog_ref[...]) * _softplus(ba + dtb_ref[...])
    r = lax.broadcasted_iota(jnp.int32, (LANES, LANES), 0)
    c = lax.broadcasted_iota(jnp.int32, (LANES, LANES), 1)
    tril = jnp.where((r // GDN_CHUNK == c // GDN_CHUNK) & (c <= r), 1.0, 0.0).astype(F32)
    parts = [
        jnp.dot(tril, g[i * LANES:(i + 1) * LANES], precision=lax.Precision.HIGHEST, preferred_element_type=F32)
        for i in range(S // LANES)
    ]
    gc = jnp.concatenate(parts, axis=0)
    lane = lax.broadcasted_iota(jnp.int32, ba.shape, 1)
    col = jnp.where(lane < n_heads, _sigmoid(ba), gc)
    col_ref[0] = col
    row_ref[0] = col.T


def _gdn_prep(ba, alog_l, dtb_l, *, n_heads):
    B, S, _ = ba.shape
    return pl.pallas_call(
        functools.partial(_gdn_prep_kernel, n_heads=n_heads),
        grid=(B,),
        in_specs=[
            pl.BlockSpec((1, S, LANES), lambda b: (b, 0, 0)),
            pl.BlockSpec((1, LANES), lambda b: (0, 0)),
            pl.BlockSpec((1, LANES), lambda b: (0, 0)),
        ],
        out_specs=[
            pl.BlockSpec((1, S, LANES), lambda b: (b, 0, 0)),
            pl.BlockSpec((1, LANES, S), lambda b: (b, 0, 0)),
        ],
        out_shape=[
            jax.ShapeDtypeStruct((B, S, LANES), F32),
            jax.ShapeDtypeStruct((B, LANES, S), F32),
        ],
        compiler_params=_params(1),
        name="gdn_prep",
    )(ba, alog_l, dtb_l)


def _gdn_kernel(q_ref, k_ref, v_ref, z_ref, cwq_ref, cwk_ref, cwv_ref, col_ref, row_ref, gain_ref, o_ref,
                beta_s, gc_s, pg_s, qn_s, on_s, egl_s, xpad_s, *, n_heads, heads_per_step, chunks_per_iter):
    C = GDN_CHUNK
    PAD = 8
    S = col_ref.shape[1]
    n_chunks = S // C
    hp = heads_per_step
    h0 = pl.program_id(1) * hp

    colv = col_ref[0]
    lane = lax.broadcasted_iota(jnp.int32, colv.shape, 1)
    for e in range(hp):
        beta_s[e] = jnp.broadcast_to(
            jnp.sum(jnp.where(lane == h0 + e, colv, 0.0), axis=1, keepdims=True), colv.shape)
        gc_s[e] = jnp.broadcast_to(
            jnp.sum(jnp.where(lane == h0 + e + n_heads, colv, 0.0), axis=1, keepdims=True), colv.shape)

    ri = lax.broadcasted_iota(jnp.int32, (C, C), 0)
    ci = lax.broadcasted_iota(jnp.int32, (C, C), 1)
    tril = ci <= ri
    strict = ci < ri

    for e in range(hp):
        for t, x_ref in enumerate((q_ref, k_ref, v_ref)):
            xpad_s[e, t, 0:PAD, :] = jnp.zeros((PAD, LANES), F32)
            xpad_s[e, t, PAD:, :] = x_ref[0, e]

    def conv_silu(t, cw_ref, e, r0):
        cw = cw_ref[:, e * LANES:(e + 1) * LANES]
        y = cw[GDN_CONV - 1:GDN_CONV] * xpad_s[e, t, pl.ds(r0 + PAD, C), :]
        for i in range(GDN_CONV - 1):
            y = y + cw[i:i + 1] * xpad_s[e, t, pl.ds(r0 + PAD - (GDN_CONV - 1) + i, C), :]
        return y * _sigmoid(y)

    def l2n(x):
        return x * lax.rsqrt(jnp.sum(x * x, axis=-1, keepdims=True) + L2_EPS)

    def prep_body(e, it):
        ns = [it * chunks_per_iter + s for s in range(chunks_per_iter)]
        r0s = [pl.multiple_of(n * C, C) for n in ns]
        q = [l2n(conv_silu(0, cwq_ref, e, r0)) * (HEAD_DIM ** -0.5) for r0 in r0s]
        k = [l2n(conv_silu(1, cwk_ref, e, r0)) for r0 in r0s]
        v = [conv_silu(2, cwv_ref, e, r0) for r0 in r0s]
        beta = [beta_s[e, pl.ds(r0, C), :] for r0 in r0s]
        gcb = [gc_s[e, pl.ds(r0, C), :] for r0 in r0s]
        decay = [jnp.exp(jnp.where(tril, g[:, :C] - row_ref[0, e, pl.ds(n, 1), :], NEG_BIG))
                 for n, g in zip(ns, gcb)]
        k_beta = [a * b for a, b in zip(k, beta)]
        qk = [_dot_nt(jnp.concatenate([a, b], axis=0).astype(BF16), c.astype(BF16))
              for a, b, c in zip(q, k_beta, k)]
        intra = [jnp.where(tril, a[:C] * d, 0.0).astype(BF16) for a, d in zip(qk, decay)]
        eg = [jnp.exp(g) for g in gcb]
        x = [jnp.concatenate([a * b, c * d], axis=1) for a, b, c, d in zip(v, beta, k_beta, eg)]
        m = [jnp.where(strict, -(a[C:] * d), 0.0) for a, d in zip(qk, decay)]
        n_steps = C.bit_length() - 1
        for j in range(n_steps):
            mb = [a.astype(BF16) for a in m]
            x = [a + _dot(b, a.astype(BF16)) for a, b in zip(x, mb)]
            if j + 1 < n_steps:
                m = [_dot(b, b) for b in mb]
        gl = [g[C - 1:C, :] for g in gcb]
        xb = [a.astype(BF16) for a in x]
        kd = [(a * jnp.exp(l - g)).astype(BF16) for a, l, g in zip(k, gl, gcb)]
        kuw = [_dot_tn(a, b) for a, b in zip(kd, xb)]
        auw = [_dot(a, b) for a, b in zip(intra, xb)]
        for i, n in enumerate(ns):
            p0 = pl.multiple_of(n * (HEAD_DIM + C), HEAD_DIM + C)
            pg_s[e, pl.ds(p0, HEAD_DIM), :] = kuw[i][:, HEAD_DIM:].astype(BF16)
            pg_s[e, pl.ds(p0 + HEAD_DIM, C), :] = (q[i] * eg[i] - auw[i][:, HEAD_DIM:]).astype(BF16)
            qn_s[e, pl.ds(pl.multiple_of(n * HEAD_DIM, HEAD_DIM), HEAD_DIM), :] = kuw[i][:, :HEAD_DIM]
            on_s[e, pl.ds(r0s[i], C), :] = auw[i][:, :HEAD_DIM]
            egl_s[e, pl.ds(pl.multiple_of(n * 8, 8), 8), :] = jnp.broadcast_to(jnp.exp(gl[i]), (8, HEAD_DIM))

    for e in range(hp):
        def body(it, carry, e=e):
            prep_body(e, it)
            return carry

        lax.fori_loop(0, n_chunks // chunks_per_iter, body, 0)

    gain = gain_ref[...]

    def emit(n, o_raw):
        r0 = pl.multiple_of(n * C, C)
        for e in range(hp):
            z = z_ref[0, e, pl.ds(r0, C), :]
            o_ref[0, pl.ds(r0, C), e * LANES:(e + 1) * LANES] = (
                _rms(o_raw[e], gain) * (z * _sigmoid(z))).astype(o_ref.dtype)

    def step(n, carry):
        states, o_prev = carry
        r0 = pl.multiple_of(n * C, C)
        p0 = pl.multiple_of(n * (HEAD_DIM + C), HEAD_DIM + C)
        n0 = pl.multiple_of(n * HEAD_DIM, HEAD_DIM)
        ps = [_dot(pg_s[e, pl.ds(p0, HEAD_DIM + C), :], states[e].astype(BF16))
              for e in range(hp)]
        emit(jnp.maximum(n - 1, 0), o_prev)
        new_states = []
        for e in range(hp):
            egl = egl_s[e, pl.ds(pl.multiple_of(n * 8, 8), 1), :]
            new_states.append(states[e] * jnp.broadcast_to(egl, states[e].shape)
                              + (qn_s[e, pl.ds(n0, HEAD_DIM), :] - ps[e][:HEAD_DIM]))
        o_raw = tuple(ps[e][HEAD_DIM:] + on_s[e, pl.ds(r0, C), :] for e in range(hp))
        return tuple(new_states), o_raw

    init = (tuple(jnp.zeros((HEAD_DIM, HEAD_DIM), F32) for _ in range(hp)),
            tuple(jnp.zeros((C, HEAD_DIM), F32) for _ in range(hp)))
    _, o_last = lax.fori_loop(0, n_chunks, step, init)
    emit(n_chunks - 1, o_last)


def _gdn(proj, conv_w, col, row, o_gain, *, n_heads, chunks_per_iter=32):
    B, _, S, _ = proj.shape
    H = n_heads
    hp = 2 if H % 2 == 0 else 1
    n_chunks = S // GDN_CHUNK

    def slab(off):
        return pl.BlockSpec((1, hp, S, LANES), lambda b, h: (b, off // hp + h, 0, 0))

    def cw(off):
        return pl.BlockSpec((GDN_CONV, hp * LANES), lambda b, h: (0, off // hp + h))

    def per_head(rows, dtype):
        return pltpu.VMEM((hp, rows, HEAD_DIM), dtype)

    return pl.pallas_call(
        functools.partial(_gdn_kernel, n_heads=H, heads_per_step=hp,
                          chunks_per_iter=_tile(n_chunks, chunks_per_iter)),
        grid=(B, H // hp),
        in_specs=[
            slab(0), slab(H), slab(2 * H), slab(3 * H),
            cw(0), cw(H), cw(2 * H),
            pl.BlockSpec((1, S, LANES), lambda b, h: (b, 0, 0)),
            pl.BlockSpec((1, hp, n_chunks, GDN_CHUNK), lambda b, h: (b, H // hp + h, 0, 0)),
            pl.BlockSpec((1, LANES), lambda b, h: (0, 0)),
        ],
        out_specs=pl.BlockSpec((1, S, hp * LANES), lambda b, h: (b, 0, h)),
        out_shape=jax.ShapeDtypeStruct((B, S, H * HEAD_DIM), BF16),
        scratch_shapes=[
            per_head(S, F32),
            per_head(S, F32),
            per_head(n_chunks * (HEAD_DIM + GDN_CHUNK), BF16),
            per_head(n_chunks * HEAD_DIM, F32),
            per_head(S, F32),
            per_head(n_chunks * 8, F32),
            pltpu.VMEM((hp, 3, S + 8, LANES), F32),
        ],
        compiler_params=_params(2),
        name="gdn",
    )(proj, proj, proj, proj, conv_w, conv_w, conv_w, col, row, o_gain)


def _sb_kernel(q_ref, k_ref, v_ref, qg_ref, kg_ref, og_ref, o_ref, qs, ks, vs, *, group):
    T = SB_BLOCK
    G = group
    Q = G * T
    S = q_ref.shape[2]
    qs[...] = _rms(q_ref[0, 0], qg_ref[...]).astype(BF16)
    ks[...] = _rms(k_ref[0, 0], kg_ref[...]).astype(BF16)
    vs[...] = v_ref[0, 0].astype(BF16)
    scale = HEAD_DIM ** -0.5 * LOG2E
    og = og_ref[...]
    sign_bit = jnp.int32(-2 ** 31)

    ri = lax.broadcasted_iota(jnp.int32, (T, T), 0)
    ci = lax.broadcasted_iota(jnp.int32, (T, T), 1)
    past = ci < ri
    jj = lax.broadcasted_iota(jnp.int32, (2 * T, 2 * T), 0) % T
    ss = lax.broadcasted_iota(jnp.int32, (2 * T, 2 * T), 1)
    uneg = jnp.where((ss >= T) | (jj >= ss), -1.0, 0.0).astype(BF16)

    def stage_a(qb, k0, diag):
        z = _dot_nt(qb, ks[k0:k0 + Q, :]) * scale
        zs, lhs = [], []
        for d in range(G):
            zd = z[d * T if diag else 0:, d * T:(d + 1) * T]
            neg_abs = lax.bitcast_convert_type(lax.bitcast_convert_type(zd, jnp.int32) | sign_bit, F32)
            sp = jnp.maximum(zd, 0.0) + jnp.log(1.0 + jnp.exp2(neg_abs)) * LOG2E
            if diag:
                top = jnp.where(past, sp[:T], 0.0)
                sp = top if d == G - 1 else jnp.concatenate([top, sp[T:]], axis=0)
            hi = sp.astype(BF16)
            lo = (sp - hi.astype(F32)).astype(BF16)
            zs.append(zd)
            lhs.append(jnp.concatenate([hi, lo], axis=1))
        comb = _dot(lhs[0] if G == 1 else jnp.concatenate(lhs, axis=0), uneg)
        return zs, comb

    def stage_b(zs, comb, carry, k0, diag):
        offs = [0]
        for d in range(G):
            offs.append(offs[-1] + zs[d].shape[0])
        w_blocks = [None] * G
        for d in range(G - 1, -1, -1):
            lo_r = d * T if diag else 0
            cd = comb[offs[d]:offs[d + 1]]
            c_rows = carry[lo_r:]
            logw = zs[d] + c_rows + cd[:, :T]
            if diag:
                top = jnp.where(past, logw[:T], NEG_BIG)
                logw = top if d == G - 1 else jnp.concatenate([top, logw[T:]], axis=0)
            w = jnp.exp2(logw).astype(BF16)
            if lo_r:
                w = jnp.concatenate([jnp.zeros((lo_r, T), BF16), w], axis=0)
            w_blocks[d] = w
            c_new = c_rows + cd[:, T:]
            carry = c_new if lo_r == 0 else jnp.concatenate([carry[:lo_r], c_new], axis=0)
        wt = w_blocks[0] if G == 1 else jnp.concatenate(w_blocks, axis=1)
        return carry, _dot(wt, vs[k0:k0 + Q, :])

    tiles = []
    for i in range(S // Q):
        tiles.append((i, i * Q, True))
        tiles.extend((i, (i - 1 - t) * Q, False) for t in range(i))
    nxt = stage_a(qs[0:Q, :], 0, True)
    carry = acc = None
    for n, (i, k0, diag) in enumerate(tiles):
        cur = nxt
        if n + 1 < len(tiles):
            i2, k2, diag2 = tiles[n + 1]
            nxt = stage_a(qs[i2 * Q:(i2 + 1) * Q, :], k2, diag2)
        if diag:
            carry, acc = jnp.zeros((Q, T), F32), None
        carry, contrib = stage_b(cur[0], cur[1], carry, k0, diag)
        acc = contrib if acc is None else acc + contrib
        if n + 1 == len(tiles) or tiles[n + 1][2]:
            o_ref[0, i * Q:(i + 1) * Q, :] = _rms(acc, og).astype(o_ref.dtype)


def _sb(proj, q_gain, k_gain, o_gain, *, n_heads, slab0):
    B, _, S, _ = proj.shape
    H = n_heads
    group = _tile(S // SB_BLOCK, 4)

    def slab(off):
        return pl.BlockSpec((1, 1, S, LANES), lambda b, h: (b, off + h, 0, 0))

    gain = pl.BlockSpec((1, LANES), lambda b, h: (0, 0))
    return pl.pallas_call(
        functools.partial(_sb_kernel, group=group),
        grid=(B, H),
        in_specs=[slab(slab0), slab(slab0 + H), slab(slab0 + 2 * H), gain, gain, gain],
        out_specs=pl.BlockSpec((1, S, LANES), lambda b, h: (b, 0, h)),
        out_shape=jax.ShapeDtypeStruct((B, S, H * HEAD_DIM), BF16),
        scratch_shapes=[pltpu.VMEM((S, HEAD_DIM), BF16)] * 3,
        compiler_params=_params(2),
        name="sb",
    )(proj, proj, proj, q_gain, k_gain, o_gain)


def _outproj_kernel(yg_ref, ys_ref, w_ref, x_ref, o_ref):
    kg = yg_ref.shape[2]
    o_ref[0] = x_ref[0] + _dot(yg_ref[0], w_ref[:kg, :]) + _dot(ys_ref[0], w_ref[kg:, :])


def _outproj(yg, ys, w, layer, x, *, tm, tn):
    B, S, D = x.shape
    Kg, Ks = yg.shape[2], ys.shape[2]
    return pl.pallas_call(
        _outproj_kernel,
        grid=(B, S // tm, D // tn),
        in_specs=[
            pl.BlockSpec((1, tm, Kg), lambda b, i, j: (b, i, 0)),
            pl.BlockSpec((1, tm, Ks), lambda b, i, j: (b, i, 0)),
            pl.BlockSpec((None, Kg + Ks, tn), lambda b, i, j: (layer, 0, j)),
            pl.BlockSpec((1, tm, tn), lambda b, i, j: (b, i, j)),
        ],
        out_specs=pl.BlockSpec((1, tm, tn), lambda b, i, j: (b, i, j)),
        out_shape=jax.ShapeDtypeStruct((B, S, D), F32),
        compiler_params=_params(3),
        name="outproj",
    )(yg, ys, w, x)


def _ffn_up_kernel(x_ref, halo_ref, g_ref, wg_ref, wv_ref, cg_ref, cv_ref, bg_ref, bv_ref, o_ref,
                   h_scr, ug_scr, uv_scr):
    tm = x_ref.shape[1]

    @pl.when(pl.program_id(2) == 0)
    def _():
        gain = g_ref[...]
        hh = _rms(halo_ref[0], gain)
        hh = jnp.where(pl.program_id(1) > 0, hh, 0.0)
        h_scr[:HALO, :] = hh.astype(BF16)
        h_scr[HALO:, :] = _rms(x_ref[0], gain).astype(BF16)

    hb = h_scr[...]
    ug_scr[...] = _dot(hb, wg_ref[...])
    uv_scr[...] = _dot(hb, wv_ref[...])

    def conv(u_scr, c_ref, b_ref):
        cw = c_ref[...]
        y = b_ref[...] + cw[FFN_CONV - 1:FFN_CONV] * u_scr[pl.ds(HALO, tm), :]
        for i in range(FFN_CONV - 1):
            y = y + cw[i:i + 1] * u_scr[pl.ds(HALO - (FFN_CONV - 1) + i, tm), :]
        return y

    gate = conv(ug_scr, cg_ref, bg_ref)
    val = conv(uv_scr, cv_ref, bv_ref)
    o_ref[0] = (gate * _sigmoid(gate) * val).astype(o_ref.dtype)


def _ffn_up(x, gain, w_up, layer, conv_w, bias, *, tm, tn):
    B, S, D = x.shape
    F = w_up.shape[2] // 2
    nf = F // tn
    hb = tm // HALO
    return pl.pallas_call(
        _ffn_up_kernel,
        grid=(B, S // tm, nf),
        in_specs=[
            pl.BlockSpec((1, tm, D), lambda b, i, j: (b, i, 0)),
            pl.BlockSpec((1, HALO, D), lambda b, i, j: (b, jnp.maximum(i * hb - 1, 0), 0)),
            pl.BlockSpec((1, D), lambda b, i, j: (0, 0)),
            pl.BlockSpec((None, D, tn), lambda b, i, j: (layer, 0, j)),
            pl.BlockSpec((None, D, tn), lambda b, i, j: (layer, 0, nf + j)),
            pl.BlockSpec((FFN_CONV, tn), lambda b, i, j: (0, j)),
            pl.BlockSpec((FFN_CONV, tn), lambda b, i, j: (0, nf + j)),
            pl.BlockSpec((1, tn), lambda b, i, j: (0, j)),
            pl.BlockSpec((1, tn), lambda b, i, j: (0, nf + j)),
        ],
        out_specs=pl.BlockSpec((1, tm, tn), lambda b, i, j: (b, i, j)),
        out_shape=jax.ShapeDtypeStruct((B, S, F), BF16),
        scratch_shapes=[
            pltpu.VMEM((tm + HALO, D), BF16),
            pltpu.VMEM((tm + HALO, tn), F32),
            pltpu.VMEM((tm + HALO, tn), F32),
        ],
        compiler_params=_params(3),
        name="ffn_up",
    )(x, x, gain, w_up, w_up, conv_w, conv_w, bias, bias)


def _ffn_down_kernel(a_ref, w_ref, x_ref, o_ref):
    o_ref[0] = x_ref[0] + _dot(a_ref[0], w_ref[...])


def _ffn_down(act, w, layer, x, *, tm, tn):
    B, S, D = x.shape
    F = act.shape[2]
    return pl.pallas_call(
        _ffn_down_kernel,
        grid=(B, S // tm, D // tn),
        in_specs=[
            pl.BlockSpec((1, tm, F), lambda b, i, j: (b, i, 0)),
            pl.BlockSpec((None, F, tn), lambda b, i, j: (layer, 0, j)),
            pl.BlockSpec((1, tm, tn), lambda b, i, j: (b, i, j)),
        ],
        out_specs=pl.BlockSpec((1, tm, tn), lambda b, i, j: (b, i, j)),
        out_shape=jax.ShapeDtypeStruct((B, S, D), F32),
        compiler_params=_params(3),
        name="ffn_down",
    )(act, w, x)


def _tile(n, pref):
    t = min(n, pref)
    while n % t:
        t //= 2
    return t


def kernel(x, attn_norm, w_in, gdn_conv, gdn_a_log, gdn_dt_bias, gdn_o_norm, sb_q_norm, sb_k_norm, sb_o_norm,
           w_out, ffn_norm, w_up, ffn_conv, ffn_conv_bias, w_down):
    B, S, D = x.shape
    depth = w_in.shape[0]
    H = gdn_a_log.shape[1]
    W = H * HEAD_DIM
    assert w_in.shape[2] == 7 * W + 2 * H and S % LANES == 0 and 2 * H <= LANES
    F = w_down.shape[1]
    tm = _tile(S, 1024)
    tn_in = _tile(W, 1024)
    tn_d = _tile(D, 512)
    tn_o = _tile(D, 1024)
    tn_f = _tile(F, 512)

    def lane_row(v, off):
        return jnp.zeros((1, LANES), F32).at[0, off:off + v.shape[0]].set(v)

    w_in_b = w_in.astype(BF16)
    w_sb = w_in_b[..., 4 * W + 2 * H:]
    w_ba = jnp.pad(w_in[..., 4 * W:4 * W + 2 * H], ((0, 0), (0, 0), (0, LANES - 2 * H))).astype(BF16)
    w_out_b, w_up_b, w_down_b = w_out.astype(BF16), w_up.astype(BF16), w_down.astype(BF16)
    for l in range(depth):
        proj, ba = _inproj(x, attn_norm[l][None], w_in_b, 4 * W // tn_in, w_sb, w_ba, l, tm=tm, tn=tn_in)
        col, row = _gdn_prep(ba, lane_row(gdn_a_log[l], H), lane_row(gdn_dt_bias[l], H), n_heads=H)
        row = row.reshape(B, LANES, S // GDN_CHUNK, GDN_CHUNK)
        y_gdn = _gdn(proj, gdn_conv[l], col, row, gdn_o_norm[l][None], n_heads=H)
        y_sb = _sb(proj, sb_q_norm[l][None], sb_k_norm[l][None], sb_o_norm[l][None], n_heads=H, slab0=4 * H)
        x = _outproj(y_gdn, y_sb, w_out_b, l, x, tm=tm, tn=tn_o)
        act = _ffn_up(x, ffn_norm[l][None], w_up_b, l, ffn_conv[l], ffn_conv_bias[l][None], tm=tm, tn=tn_f)
        x = _ffn_down(act, w_down_b, l, x, tm=tm, tn=tn_d)
    return x
```

```python
import functools

import jax
import jax.numpy as jnp
from jax import lax
from jax.experimental import pallas as pl
from jax.experimental.pallas import tpu as pltpu

F32 = jnp.float32
BF16 = jnp.bfloat16
RMS_EPS = 1e-6
L2_EPS = 1e-6
HEAD_DIM = 128
LANES = 128
GDN_CHUNK = 64
GDN_CONV = 4
FFN_CONV = 3
SB_BLOCK = 128
HALO = 16
NEG_BIG = -1e30
LOG2E = 1.4426950408889634
VMEM_LIMIT = 56 * 1024 * 1024


def _dot(a, b):
    return jnp.dot(a, b, preferred_element_type=F32)


def _dot_nt(a, b):
    return lax.dot_general(a, b, (((1,), (1,)), ((), ())), preferred_element_type=F32)


def _dot_tn(a, b):
    return lax.dot_general(a, b, (((0,), (0,)), ((), ())), preferred_element_type=F32)


def _sigmoid(x):
    return 1.0 / (1.0 + jnp.exp(-x))


def _softplus(x):
    return jnp.maximum(x, 0.0) + jnp.log1p(jnp.exp(-jnp.abs(x)))


def _rms(x, gain):
    return (x * lax.rsqrt(jnp.mean(x * x, axis=-1, keepdims=True) + RMS_EPS)) * gain


def _params(n_axes):
    return pltpu.CompilerParams(dimension_semantics=("arbitrary",) * n_axes, vmem_limit_bytes=VMEM_LIMIT)


def _inproj_kernel(x_ref, g_ref, wa_ref, wb_ref, wba_ref, o_ref, ba_ref, h_scr, *, n_slabs, n_a):
    j = pl.program_id(2)

    @pl.when(j == 0)
    def _():
        hb = _rms(x_ref[0], g_ref[...]).astype(BF16)
        h_scr[...] = hb
        ba_ref[0] = _dot(hb, wba_ref[...])

    def emit(w_ref):
        res = _dot(h_scr[...], w_ref[...])
        for c in range(n_slabs):
            o_ref[0, c] = res[:, c * LANES:(c + 1) * LANES]

    pl.when(j < n_a)(lambda: emit(wa_ref))
    pl.when(j >= n_a)(lambda: emit(wb_ref))


def _inproj(x, gain, w_a, n_a, w_b, w_ba, layer, *, tm, tn):
    B, S, D = x.shape
    n_b = w_b.shape[2] // tn
    n_slabs = tn // LANES
    return pl.pallas_call(
        functools.partial(_inproj_kernel, n_slabs=n_slabs, n_a=n_a),
        grid=(B, S // tm, n_a + n_b),
        in_specs=[
            pl.BlockSpec((1, tm, D), lambda b, i, j: (b, i, 0)),
            pl.BlockSpec((1, D), lambda b, i, j: (0, 0)),
            pl.BlockSpec((None, D, tn), lambda b, i, j: (layer, 0, jnp.minimum(j, n_a - 1))),
            pl.BlockSpec((None, D, tn), lambda b, i, j: (layer, 0, jnp.where(j < n_a, n_b - 1, j - n_a))),
            pl.BlockSpec((None, D, LANES), lambda b, i, j: (layer, 0, 0)),
        ],
        out_specs=[
            pl.BlockSpec((1, n_slabs, tm, LANES), lambda b, i, j: (b, j, i, 0)),
            pl.BlockSpec((1, tm, LANES), lambda b, i, j: (b, i, 0)),
        ],
        out_shape=[
            jax.ShapeDtypeStruct((B, (n_a + n_b) * n_slabs, S, LANES), F32),
            jax.ShapeDtypeStruct((B, S, LANES), F32),
        ],
        scratch_shapes=[pltpu.VMEM((tm, D), BF16)],
        compiler_params=_params(3),
        name="inproj",
    )(x, gain, w_a, w_b, w_ba)


def _gdn_prep_kernel(ba_ref, alog_ref, dtb_ref, col_ref, row_ref, *, n_heads):
    ba = ba_ref[0]
    S = ba.shape[0]
    g = -jnp.exp(alog_ref[...]) * _softplus(ba + dtb_ref[...])
    r = lax.broadcasted_iota(jnp.int32, (LANES, LANES), 0)
    c = lax.broadcasted_iota(jnp.int32, (LANES, LANES), 1)
    tril = jnp.where((r // GDN_CHUNK == c // GDN_CHUNK) & (c <= r), 1.0, 0.0).astype(F32)
    parts = [
        jnp.dot(tril, g[i * LANES:(i + 1) * LANES], precision=lax.Precision.HIGHEST, preferred_element_type=F32)
        for i in range(S // LANES)
    ]
    gc = jnp.concatenate(parts, axis=0)
    lane = lax.broadcasted_iota(jnp.int32, ba.shape, 1)
    col = jnp.where(lane < n_heads, _sigmoid(ba), gc)
    col_ref[0] = col
    row_ref[0] = col.T


def _gdn_prep(ba, alog_l, dtb_l, *, n_heads):
    B, S, _ = ba.shape
    return pl.pallas_call(
        functools.partial(_gdn_prep_kernel, n_heads=n_heads),
        grid=(B,),
        in_specs=[
            pl.BlockSpec((1, S, LANES), lambda b: (b, 0, 0)),
            pl.BlockSpec((1, LANES), lambda b: (0, 0)),
            pl.BlockSpec((1, LANES), lambda b: (0, 0)),
        ],
        out_specs=[
            pl.BlockSpec((1, S, LANES), lambda b: (b, 0, 0)),
            pl.BlockSpec((1, LANES, S), lambda b: (b, 0, 0)),
        ],
        out_shape=[
            jax.ShapeDtypeStruct((B, S, LANES), F32),
            jax.ShapeDtypeStruct((B, LANES, S), F32),
        ],
        compiler_params=_params(1),
        name="gdn_prep",
    )(ba, alog_l, dtb_l)


def _gdn_kernel(q_ref, k_ref, v_ref, z_ref, cwq_ref, cwk_ref, cwv_ref, col_ref, row_ref, gain_ref, o_ref,
                beta_s, gc_s, pg_s, qn_s, on_s, egl_s, xpad_s, *, n_heads, heads_per_step, chunks_per_iter):
    C = GDN_CHUNK
    PAD = 8
    S = col_ref.shape[1]
    n_chunks = S // C
    hp = heads_per_step
    h0 = pl.program_id(1) * hp

    colv = col_ref[0]
    lane = lax.broadcasted_iota(jnp.int32, colv.shape, 1)
    for e in range(hp):
        beta_s[e] = jnp.broadcast_to(
            jnp.sum(jnp.where(lane == h0 + e, colv, 0.0), axis=1, keepdims=True), colv.shape)
        gc_s[e] = jnp.broadcast_to(
            jnp.sum(jnp.where(lane == h0 + e + n_heads, colv, 0.0), axis=1, keepdims=True), colv.shape)

    ri = lax.broadcasted_iota(jnp.int32, (C, C), 0)
    ci = lax.broadcasted_iota(jnp.int32, (C, C), 1)
    tril = ci <= ri
    strict = ci < ri

    for e in range(hp):
        for t, x_ref in enumerate((q_ref, k_ref, v_ref)):
            xpad_s[e, t, 0:PAD, :] = jnp.zeros((PAD, LANES), F32)
            xpad_s[e, t, PAD:, :] = x_ref[0, e]

    def conv_silu(t, cw_ref, e, r0):
        cw = cw_ref[:, e * LANES:(e + 1) * LANES]
        y = cw[GDN_CONV - 1:GDN_CONV] * xpad_s[e, t, pl.ds(r0 + PAD, C), :]
        for i in range(GDN_CONV - 1):
            y = y + cw[i:i + 1] * xpad_s[e, t, pl.ds(r0 + PAD - (GDN_CONV - 1) + i, C), :]
        return y * _sigmoid(y)

    def l2n(x):
        return x * lax.rsqrt(jnp.sum(x * x, axis=-1, keepdims=True) + L2_EPS)

    def prep_body(e, it):
        ns = [it * chunks_per_iter + s for s in range(chunks_per_iter)]
        r0s = [pl.multiple_of(n * C, C) for n in ns]
        q = [l2n(conv_silu(0, cwq_ref, e, r0)) * (HEAD_DIM ** -0.5) for r0 in r0s]
        k = [l2n(conv_silu(1, cwk_ref, e, r0)) for r0 in r0s]
        v = [conv_silu(2, cwv_ref, e, r0) for r0 in r0s]
        beta = [beta_s[e, pl.ds(r0, C), :] for r0 in r0s]
        gcb = [gc_s[e, pl.ds(r0, C), :] for r0 in r0s]
        decay = [jnp.exp(jnp.where(tril, g[:, :C] - row_ref[0, e, pl.ds(n, 1), :], NEG_BIG))
                 for n, g in zip(ns, gcb)]
        k_beta = [a * b for a, b in zip(k, beta)]
        qk = [_dot_nt(jnp.concatenate([a, b], axis=0).astype(BF16), c.astype(BF16))
              for a, b, c in zip(q, k_beta, k)]
        intra = [jnp.where(tril, a[:C] * d, 0.0).astype(BF16) for a, d in zip(qk, decay)]
        eg = [jnp.exp(g) for g in gcb]
        x = [jnp.concatenate([a * b, c * d], axis=1) for a, b, c, d in zip(v, beta, k_beta, eg)]
        m = [jnp.where(strict, -(a[C:] * d), 0.0) for a, d in zip(qk, decay)]
        n_steps = C.bit_length() - 1
        for j in range(n_steps):
            mb = [a.astype(BF16) for a in m]
            x = [a + _dot(b, a.astype(BF16)) for a, b in zip(x, mb)]
            if j + 1 < n_steps:
                m = [_dot(b, b) for b in mb]
        gl = [g[C - 1:C, :] for g in gcb]
        xb = [a.astype(BF16) for a in x]
        kd = [(a * jnp.exp(l - g)).astype(BF16) for a, l, g in zip(k, gl, gcb)]
        kuw = [_dot_tn(a, b) for a, b in zip(kd, xb)]
        auw = [_dot(a, b) for a, b in zip(intra, xb)]
        for i, n in enumerate(ns):
            p0 = pl.multiple_of(n * (HEAD_DIM + C), HEAD_DIM + C)
            pg_s[e, pl.ds(p0, HEAD_DIM), :] = kuw[i][:, HEAD_DIM:].astype(BF16)
            pg_s[e, pl.ds(p0 + HEAD_DIM, C), :] = (q[i] * eg[i] - auw[i][:, HEAD_DIM:]).astype(BF16)
            qn_s[e, pl.ds(pl.multiple_of(n * HEAD_DIM, HEAD_DIM), HEAD_DIM), :] = kuw[i][:, :HEAD_DIM]
            on_s[e, pl.ds(r0s[i], C), :] = auw[i][:, :HEAD_DIM]
            egl_s[e, pl.ds(pl.multiple_of(n * 8, 8), 8), :] = jnp.broadcast_to(jnp.exp(gl[i]), (8, HEAD_DIM))

    for e in range(hp):
        def body(it, carry, e=e):
            prep_body(e, it)
            return carry

        lax.fori_loop(0, n_chunks // chunks_per_iter, body, 0)

    gain = gain_ref[...]

    def emit(n, o_raw):
        r0 = pl.multiple_of(n * C, C)
        for e in range(hp):
            z = z_ref[0, e, pl.ds(r0, C), :]
            o_ref[0, pl.ds(r0, C), e * LANES:(e + 1) * LANES] = (
                _rms(o_raw[e], gain) * (z * _sigmoid(z))).astype(o_ref.dtype)

    def step(n, carry):
        states, o_prev = carry
        r0 = pl.multiple_of(n * C, C)
        p0 = pl.multiple_of(n * (HEAD_DIM + C), HEAD_DIM + C)
        n0 = pl.multiple_of(n * HEAD_DIM, HEAD_DIM)
        ps = [_dot(pg_s[e, pl.ds(p0, HEAD_DIM + C), :], states[e].astype(BF16))
              for e in range(hp)]
        emit(jnp.maximum(n - 1, 0), o_prev)
        new_states = []
        for e in range(hp):
            egl = egl_s[e, pl.ds(pl.multiple_of(n * 8, 8), 1), :]
            new_states.append(states[e] * jnp.broadcast_to(egl, states[e].shape)
                              + (qn_s[e, pl.ds(n0, HEAD_DIM), :] - ps[e][:HEAD_DIM]))
        o_raw = tuple(ps[e][HEAD_DIM:] + on_s[e, pl.ds(r0, C), :] for e in range(hp))
        return tuple(new_states), o_raw

    init = (tuple(jnp.zeros((HEAD_DIM, HEAD_DIM), F32) for _ in range(hp)),
            tuple(jnp.zeros((C, HEAD_DIM), F32) for _ in range(hp)))
    _, o_last = lax.fori_loop(0, n_chunks, step, init)
    emit(n_chunks - 1, o_last)


def _gdn(proj, conv_w, col, row, o_gain, *, n_heads, chunks_per_iter=32):
    B, _, S, _ = proj.shape
    H = n_heads
    hp = 2 if H % 2 == 0 else 1
    n_chunks = S // GDN_CHUNK

    def slab(off):
        return pl.BlockSpec((1, hp, S, LANES), lambda b, h: (b, off // hp + h, 0, 0))

    def cw(off):
        return pl.BlockSpec((GDN_CONV, hp * LANES), lambda b, h: (0, off // hp + h))

    def per_head(rows, dtype):
        return pltpu.VMEM((hp, rows, HEAD_DIM), dtype)

    return pl.pallas_call(
        functools.partial(_gdn_kernel, n_heads=H, heads_per_step=hp,
                          chunks_per_iter=_tile(n_chunks, chunks_per_iter)),
        grid=(B, H // hp),
        in_specs=[
            slab(0), slab(H), slab(2 * H), slab(3 * H),
            cw(0), cw(H), cw(2 * H),
            pl.BlockSpec((1, S, LANES), lambda b, h: (b, 0, 0)),
            pl.BlockSpec((1, hp, n_chunks, GDN_CHUNK), lambda b, h: (b, H // hp + h, 0, 0)),
            pl.BlockSpec((1, LANES), lambda b, h: (0, 0)),
        ],
        out_specs=pl.BlockSpec((1, S, hp * LANES), lambda b, h: (b, 0, h)),
        out_shape=jax.ShapeDtypeStruct((B, S, H * HEAD_DIM), BF16),
        scratch_shapes=[
            per_head(S, F32),
            per_head(S, F32),
            per_head(n_chunks * (HEAD_DIM + GDN_CHUNK), BF16),
            per_head(n_chunks * HEAD_DIM, F32),
            per_head(S, F32),
            per_head(n_chunks * 8, F32),
            pltpu.VMEM((hp, 3, S + 8, LANES), F32),
        ],
        compiler_params=_params(2),
        name="gdn",
    )(proj, proj, proj, proj, conv_w, conv_w, conv_w, col, row, o_gain)


def _sb_kernel(q_ref, k_ref, v_ref, qg_ref, kg_ref, og_ref, o_ref, qs, ks, vs, *, group):
    T = SB_BLOCK
    G = group
    Q = G * T
    S = q_ref.shape[2]
    qs[...] = _rms(q_ref[0, 0], qg_ref[...]).astype(BF16)
    ks[...] = _rms(k_ref[0, 0], kg_ref[...]).astype(BF16)
    vs[...] = v_ref[0, 0].astype(BF16)
    scale = HEAD_DIM ** -0.5 * LOG2E
    og = og_ref[...]
    sign_bit = jnp.int32(-2 ** 31)

    ri = lax.broadcasted_iota(jnp.int32, (T, T), 0)
    ci = lax.broadcasted_iota(jnp.int32, (T, T), 1)
    past = ci < ri
    jj = lax.broadcasted_iota(jnp.int32, (2 * T, 2 * T), 0) % T
    ss = lax.broadcasted_iota(jnp.int32, (2 * T, 2 * T), 1)
    uneg = jnp.where((ss >= T) | (jj >= ss), -1.0, 0.0).astype(BF16)

    def stage_a(qb, k0, diag):
        z = _dot_nt(qb, ks[k0:k0 + Q, :]) * scale
        zs, lhs = [], []
        for d in range(G):
            zd = z[d * T if diag else 0:, d * T:(d + 1) * T]
            neg_abs = lax.bitcast_convert_type(lax.bitcast_convert_type(zd, jnp.int32) | sign_bit, F32)
            sp = jnp.maximum(zd, 0.0) + jnp.log(1.0 + jnp.exp2(neg_abs)) * LOG2E
            if diag:
                top = jnp.where(past, sp[:T], 0.0)
                sp = top if d == G - 1 else jnp.concatenate([top, sp[T:]], axis=0)
            hi = sp.astype(BF16)
            lo = (sp - hi.astype(F32)).astype(BF16)
            zs.append(zd)
            lhs.append(jnp.concatenate([hi, lo], axis=1))
        comb = _dot(lhs[0] if G == 1 else jnp.concatenate(lhs, axis=0), uneg)
        return zs, comb

    def stage_b(zs, comb, carry, k0, diag):
        offs = [0]
        for d in range(G):
            offs.append(offs[-1] + zs[d].shape[0])
        w_blocks = [None] * G
        for d in range(G - 1, -1, -1):
            lo_r = d * T if diag else 0
            cd = comb[offs[d]:offs[d + 1]]
            c_rows = carry[lo_r:]
            logw = zs[d] + c_rows + cd[:, :T]
            if diag:
                top = jnp.where(past, logw[:T], NEG_BIG)
                logw = top if d == G - 1 else jnp.concatenate([top, logw[T:]], axis=0)
            w = jnp.exp2(logw).astype(BF16)
            if lo_r:
                w = jnp.concatenate([jnp.zeros((lo_r, T), BF16), w], axis=0)
            w_blocks[d] = w
            c_new = c_rows + cd[:, T:]
            carry = c_new if lo_r == 0 else jnp.concatenate([carry[:lo_r], c_new], axis=0)
        wt = w_blocks[0] if G == 1 else jnp.concatenate(w_blocks, axis=1)
        return carry, _dot(wt, vs[k0:k0 + Q, :])

    tiles = []
    for i in range(S // Q):
        tiles.append((i, i * Q, True))
        tiles.extend((i, (i - 1 - t) * Q, False) for t in range(i))
    nxt = stage_a(qs[0:Q, :], 0, True)
    carry = acc = None
    for n, (i, k0, diag) in enumerate(tiles):
        cur = nxt
        if n + 1 < len(tiles):
            i2, k2, diag2 = tiles[n + 1]
            nxt = stage_a(qs[i2 * Q:(i2 + 1) * Q, :], k2, diag2)
        if diag:
            carry, acc = jnp.zeros((Q, T), F32), None
        carry, contrib = stage_b(cur[0], cur[1], carry, k0, diag)
        acc = contrib if acc is None else acc + contrib
        if n + 1 == len(tiles) or tiles[n + 1][2]:
            o_ref[0, i * Q:(i + 1) * Q, :] = _rms(acc, og).astype(o_ref.dtype)


def _sb(proj, q_gain, k_gain, o_gain, *, n_heads, slab0):
    B, _, S, _ = proj.shape
    H = n_heads
    group = _tile(S // SB_BLOCK, 4)

    def slab(off):
        return pl.BlockSpec((1, 1, S, LANES), lambda b, h: (b, off + h, 0, 0))

    gain = pl.BlockSpec((1, LANES), lambda b, h: (0, 0))
    return pl.pallas_call(
        functools.partial(_sb_kernel, group=group),
        grid=(B, H),
        in_specs=[slab(slab0), slab(slab0 + H), slab(slab0 + 2 * H), gain, gain, gain],
        out_specs=pl.BlockSpec((1, S, LANES), lambda b, h: (b, 0, h)),
        out_shape=jax.ShapeDtypeStruct((B, S, H * HEAD_DIM), BF16),
        scratch_shapes=[pltpu.VMEM((S, HEAD_DIM), BF16)] * 3,
        compiler_params=_params(2),
        name="sb",
    )(proj, proj, proj, q_gain, k_gain, o_gain)


def _outproj_kernel(yg_ref, ys_ref, w_ref, x_ref, g_ref, o_ref, h_ref):
    kg = yg_ref.shape[2]
    y = x_ref[0] + _dot(yg_ref[0], w_ref[:kg, :]) + _dot(ys_ref[0], w_ref[kg:, :])
    o_ref[0] = y
    h_ref[0] = _rms(y, g_ref[...]).astype(h_ref.dtype)


def _outproj(yg, ys, w, layer, x, ffn_gain, *, tm):
    B, S, D = x.shape
    Kg, Ks = yg.shape[2], ys.shape[2]
    return pl.pallas_call(
        _outproj_kernel,
        grid=(B, S // tm),
        in_specs=[
            pl.BlockSpec((1, tm, Kg), lambda b, i: (b, i, 0)),
            pl.BlockSpec((1, tm, Ks), lambda b, i: (b, i, 0)),
            pl.BlockSpec((None, Kg + Ks, D), lambda b, i: (layer, 0, 0)),
            pl.BlockSpec((1, tm, D), lambda b, i: (b, i, 0)),
            pl.BlockSpec((1, D), lambda b, i: (0, 0)),
        ],
        out_specs=[
            pl.BlockSpec((1, tm, D), lambda b, i: (b, i, 0)),
            pl.BlockSpec((1, tm, D), lambda b, i: (b, i, 0)),
        ],
        out_shape=[jax.ShapeDtypeStruct((B, S, D), F32), jax.ShapeDtypeStruct((B, S, D), BF16)],
        compiler_params=_params(2),
        name="outproj",
    )(yg, ys, w, x, ffn_gain)


def _ffn_up_kernel(h_ref, halo_ref, wg_ref, wv_ref, cg_ref, cv_ref, bg_ref, bv_ref, o_ref,
                   h_scr, ug_scr, uv_scr):
    tm = h_ref.shape[1]

    @pl.when(pl.program_id(2) == 0)
    def _():
        h_scr[:HALO, :] = jnp.where(pl.program_id(1) > 0, halo_ref[0], jnp.zeros_like(halo_ref[0]))
        h_scr[HALO:, :] = h_ref[0]

    hb = h_scr[...]
    ug_scr[...] = _dot(hb, wg_ref[...])
    uv_scr[...] = _dot(hb, wv_ref[...])

    def conv(u_scr, c_ref, b_ref):
        cw = c_ref[...]
        y = b_ref[...] + cw[FFN_CONV - 1:FFN_CONV] * u_scr[pl.ds(HALO, tm), :]
        for i in range(FFN_CONV - 1):
            y = y + cw[i:i + 1] * u_scr[pl.ds(HALO - (FFN_CONV - 1) + i, tm), :]
        return y

    gate = conv(ug_scr, cg_ref, bg_ref)
    val = conv(uv_scr, cv_ref, bv_ref)
    o_ref[0] = (gate * _sigmoid(gate) * val).astype(o_ref.dtype)


def _ffn_up(h, w_up, layer, conv_w, bias, *, tm, tn):
    B, S, D = h.shape
    F = w_up.shape[2] // 2
    nf = F // tn
    hb = tm // HALO
    return pl.pallas_call(
        _ffn_up_kernel,
        grid=(B, S // tm, nf),
        in_specs=[
            pl.BlockSpec((1, tm, D), lambda b, i, j: (b, i, 0)),
            pl.BlockSpec((1, HALO, D), lambda b, i, j: (b, jnp.maximum(i * hb - 1, 0), 0)),
            pl.BlockSpec((None, D, tn), lambda b, i, j: (layer, 0, j)),
            pl.BlockSpec((None, D, tn), lambda b, i, j: (layer, 0, nf + j)),
            pl.BlockSpec((FFN_CONV, tn), lambda b, i, j: (0, j)),
            pl.BlockSpec((FFN_CONV, tn), lambda b, i, j: (0, nf + j)),
            pl.BlockSpec((1, tn), lambda b, i, j: (0, j)),
            pl.BlockSpec((1, tn), lambda b, i, j: (0, nf + j)),
        ],
        out_specs=pl.BlockSpec((1, tm, tn), lambda b, i, j: (b, i, j)),
        out_shape=jax.ShapeDtypeStruct((B, S, F), BF16),
        scratch_shapes=[
            pltpu.VMEM((tm + HALO, D), BF16),
            pltpu.VMEM((tm + HALO, tn), F32),
            pltpu.VMEM((tm + HALO, tn), F32),
        ],
        compiler_params=_params(3),
        name="ffn_up",
    )(h, h, w_up, w_up, conv_w, conv_w, bias, bias)


def _ffn_down_kernel(a_ref, w_ref, x_ref, o_ref):
    o_ref[0] = x_ref[0] + _dot(a_ref[0], w_ref[...])


def _ffn_down(act, w, layer, x, *, tm, tn):
    B, S, D = x.shape
    F = act.shape[2]
    return pl.pallas_call(
        _ffn_down_kernel,
        grid=(B, S // tm, D // tn),
        in_specs=[
            pl.BlockSpec((1, tm, F), lambda b, i, j: (b, i, 0)),
            pl.BlockSpec((None, F, tn), lambda b, i, j: (layer, 0, j)),
            pl.BlockSpec((1, tm, tn), lambda b, i, j: (b, i, j)),
        ],
        out_specs=pl.BlockSpec((1, tm, tn), lambda b, i, j: (b, i, j)),
        out_shape=jax.ShapeDtypeStruct((B, S, D), F32),
        compiler_params=_params(3),
        name="ffn_down",
    )(act, w, x)


def _tile(n, pref):
    t = min(n, pref)
    while n % t:
        t //= 2
    return t


def kernel(x, attn_norm, w_in, gdn_conv, gdn_a_log, gdn_dt_bias, gdn_o_norm, sb_q_norm, sb_k_norm, sb_o_norm,
           w_out, ffn_norm, w_up, ffn_conv, ffn_conv_bias, w_down):
    B, S, D = x.shape
    depth = w_in.shape[0]
    H = gdn_a_log.shape[1]
    W = H * HEAD_DIM
    assert w_in.shape[2] == 7 * W + 2 * H and S % LANES == 0 and 2 * H <= LANES
    F = w_down.shape[1]
    tm = _tile(S, 1024)
    tn_in = _tile(W, 1024)
    tn_d = _tile(D, 512)
    tm_o = _tile(S, 512)
    tn_f = _tile(F, 512)

    def lane_row(v, off):
        return jnp.zeros((1, LANES), F32).at[0, off:off + v.shape[0]].set(v)

    w_in_b = w_in.astype(BF16)
    w_sb = w_in_b[..., 4 * W + 2 * H:]
    w_ba = jnp.pad(w_in[..., 4 * W:4 * W + 2 * H], ((0, 0), (0, 0), (0, LANES - 2 * H))).astype(BF16)
    w_out_b, w_up_b, w_down_b = w_out.astype(BF16), w_up.astype(BF16), w_down.astype(BF16)
    for l in range(depth):
        proj, ba = _inproj(x, attn_norm[l][None], w_in_b, 4 * W // tn_in, w_sb, w_ba, l, tm=tm, tn=tn_in)
        col, row = _gdn_prep(ba, lane_row(gdn_a_log[l], H), lane_row(gdn_dt_bias[l], H), n_heads=H)
        row = row.reshape(B, LANES, S // GDN_CHUNK, GDN_CHUNK)
        y_gdn = _gdn(proj, gdn_conv[l], col, row, gdn_o_norm[l][None], n_heads=H)
        y_sb = _sb(proj, sb_q_norm[l][None], sb_k_norm[l][None], sb_o_norm[l][None], n_heads=H, slab0=4 * H)
        x, h_ffn = _outproj(y_gdn, y_sb, w_out_b, l, x, ffn_norm[l][None], tm=tm_o)
        act = _ffn_up(h_ffn, w_up_b, l, ffn_conv[l], ffn_conv_bias[l][None], tm=tm, tn=tn_f)
        x = _ffn_down(act, w_down_b, l, x, tm=tm, tn=tn_d)
    return x
```

```python
import functools

import jax
import jax.numpy as jnp
from jax import lax
from jax.experimental import pallas as pl
from jax.experimental.pallas import tpu as pltpu

F32 = jnp.float32
BF16 = jnp.bfloat16
RMS_EPS = 1e-6
L2_EPS = 1e-6
HEAD_DIM = 128
LANES = 128
GDN_CHUNK = 64
GDN_CONV = 4
FFN_CONV = 3
SB_BLOCK = 128
HALO = 16
NEG_BIG = -1e30
LOG2E = 1.4426950408889634
VMEM_LIMIT = 56 * 1024 * 1024


def _dot(a, b):
    return jnp.dot(a, b, preferred_element_type=F32)


def _dot_nt(a, b):
    return lax.dot_general(a, b, (((1,), (1,)), ((), ())), preferred_element_type=F32)


def _dot_tn(a, b):
    return lax.dot_general(a, b, (((0,), (0,)), ((), ())), preferred_element_type=F32)


def _sigmoid(x):
    return 1.0 / (1.0 + jnp.exp(-x))


def _softplus(x):
    return jnp.maximum(x, 0.0) + jnp.log1p(jnp.exp(-jnp.abs(x)))


def _rms(x, gain):
    return (x * lax.rsqrt(jnp.mean(x * x, axis=-1, keepdims=True) + RMS_EPS)) * gain


def _params(n_axes):
    return pltpu.CompilerParams(dimension_semantics=("arbitrary",) * n_axes, vmem_limit_bytes=VMEM_LIMIT)


def _inproj_kernel(x_ref, g_ref, wa_ref, wb_ref, wba_ref, o_ref, ba_ref, h_scr, *, n_slabs, n_a):
    j = pl.program_id(2)

    @pl.when(j == 0)
    def _():
        hb = _rms(x_ref[0], g_ref[...]).astype(BF16)
        h_scr[...] = hb
        ba_ref[0] = _dot(hb, wba_ref[...])

    def emit(w_ref):
        res = _dot(h_scr[...], w_ref[...])
        for c in range(n_slabs):
            o_ref[0, c] = res[:, c * LANES:(c + 1) * LANES]

    pl.when(j < n_a)(lambda: emit(wa_ref))
    pl.when(j >= n_a)(lambda: emit(wb_ref))


def _inproj(x, gain, w_a, n_a, w_b, w_ba, layer, *, tm, tn):
    B, S, D = x.shape
    n_b = w_b.shape[2] // tn
    n_slabs = tn // LANES
    return pl.pallas_call(
        functools.partial(_inproj_kernel, n_slabs=n_slabs, n_a=n_a),
        grid=(B, S // tm, n_a + n_b),
        in_specs=[
            pl.BlockSpec((1, tm, D), lambda b, i, j: (b, i, 0)),
            pl.BlockSpec((1, D), lambda b, i, j: (0, 0)),
            pl.BlockSpec((None, D, tn), lambda b, i, j: (layer, 0, jnp.minimum(j, n_a - 1))),
            pl.BlockSpec((None, D, tn), lambda b, i, j: (layer, 0, jnp.where(j < n_a, n_b - 1, j - n_a))),
            pl.BlockSpec((None, D, LANES), lambda b, i, j: (layer, 0, 0)),
        ],
        out_specs=[
            pl.BlockSpec((1, n_slabs, tm, LANES), lambda b, i, j: (b, j, i, 0)),
            pl.BlockSpec((1, tm, LANES), lambda b, i, j: (b, i, 0)),
        ],
        out_shape=[
            jax.ShapeDtypeStruct((B, (n_a + n_b) * n_slabs, S, LANES), F32),
            jax.ShapeDtypeStruct((B, S, LANES), F32),
        ],
        scratch_shapes=[pltpu.VMEM((tm, D), BF16)],
        compiler_params=_params(3),
        name="inproj",
    )(x, gain, w_a, w_b, w_ba)


def _gdn_prep_kernel(ba_ref, alog_ref, dtb_ref, col_ref, row_ref, *, n_heads):
    ba = ba_ref[0]
    S = ba.shape[0]
    g = -jnp.exp(alog_ref[...]) * _softplus(ba + dtb_ref[...])
    r = lax.broadcasted_iota(jnp.int32, (LANES, LANES), 0)
    c = lax.broadcasted_iota(jnp.int32, (LANES, LANES), 1)
    tril = jnp.where((r // GDN_CHUNK == c // GDN_CHUNK) & (c <= r), 1.0, 0.0).astype(F32)
    parts = [
        jnp.dot(tril, g[i * LANES:(i + 1) * LANES], precision=lax.Precision.HIGHEST, preferred_element_type=F32)
        for i in range(S // LANES)
    ]
    gc = jnp.concatenate(parts, axis=0)
    lane = lax.broadcasted_iota(jnp.int32, ba.shape, 1)
    col = jnp.where(lane < n_heads, _sigmoid(ba), gc)
    col_ref[0] = col
    row_ref[0] = col.T


def _gdn_prep(ba, alog_l, dtb_l, *, n_heads):
    B, S, _ = ba.shape
    return pl.pallas_call(
        functools.partial(_gdn_prep_kernel, n_heads=n_heads),
        grid=(B,),
        in_specs=[
            pl.BlockSpec((1, S, LANES), lambda b: (b, 0, 0)),
            pl.BlockSpec((1, LANES), lambda b: (0, 0)),
            pl.BlockSpec((1, LANES), lambda b: (0, 0)),
        ],
        out_specs=[
            pl.BlockSpec((1, S, LANES), lambda b: (b, 0, 0)),
            pl.BlockSpec((1, LANES, S), lambda b: (b, 0, 0)),
        ],
        out_shape=[
            jax.ShapeDtypeStruct((B, S, LANES), F32),
            jax.ShapeDtypeStruct((B, LANES, S), F32),
        ],
        compiler_params=_params(1),
        name="gdn_prep",
    )(ba, alog_l, dtb_l)


def _gdn_kernel(q_ref, k_ref, v_ref, z_ref, cwq_ref, cwk_ref, cwv_ref, col_ref, row_ref, gain_ref, o_ref,
                beta_s, gc_s, pg_s, qn_s, on_s, egl_s, xpad_s, *, n_heads, heads_per_step, chunks_per_iter):
    C = GDN_CHUNK
    PAD = 8
    S = col_ref.shape[1]
    n_chunks = S // C
    hp = heads_per_step
    h0 = pl.program_id(1) * hp

    colv = col_ref[0]
    lane = lax.broadcasted_iota(jnp.int32, colv.shape, 1)
    for e in range(hp):
        beta_s[e] = jnp.broadcast_to(
            jnp.sum(jnp.where(lane == h0 + e, colv, 0.0), axis=1, keepdims=True), colv.shape)
        gc_s[e] = jnp.broadcast_to(
            jnp.sum(jnp.where(lane == h0 + e + n_heads, colv, 0.0), axis=1, keepdims=True), colv.shape)

    ri = lax.broadcasted_iota(jnp.int32, (C, C), 0)
    ci = lax.broadcasted_iota(jnp.int32, (C, C), 1)
    tril = ci <= ri
    strict = ci < ri

    for e in range(hp):
        for t, x_ref in enumerate((q_ref, k_ref, v_ref)):
            xpad_s[e, t, 0:PAD, :] = jnp.zeros((PAD, LANES), F32)
            xpad_s[e, t, PAD:, :] = x_ref[0, e]

    def conv_silu(t, cw_ref, e, r0):
        cw = cw_ref[:, e * LANES:(e + 1) * LANES]
        y = cw[GDN_CONV - 1:GDN_CONV] * xpad_s[e, t, pl.ds(r0 + PAD, C), :]
        for i in range(GDN_CONV - 1):
            y = y + cw[i:i + 1] * xpad_s[e, t, pl.ds(r0 + PAD - (GDN_CONV - 1) + i, C), :]
        return y * _sigmoid(y)

    def l2n(x):
        return x * lax.rsqrt(jnp.sum(x * x, axis=-1, keepdims=True) + L2_EPS)

    def prep_body(e, it):
        ns = [it * chunks_per_iter + s for s in range(chunks_per_iter)]
        r0s = [pl.multiple_of(n * C, C) for n in ns]
        q = [l2n(conv_silu(0, cwq_ref, e, r0)) * (HEAD_DIM ** -0.5) for r0 in r0s]
        k = [l2n(conv_silu(1, cwk_ref, e, r0)) for r0 in r0s]
        v = [conv_silu(2, cwv_ref, e, r0) for r0 in r0s]
        beta = [beta_s[e, pl.ds(r0, C), :] for r0 in r0s]
        gcb = [gc_s[e, pl.ds(r0, C), :] for r0 in r0s]
        decay = [jnp.exp(jnp.where(tril, g[:, :C] - row_ref[0, e, pl.ds(n, 1), :], NEG_BIG))
                 for n, g in zip(ns, gcb)]
        k_beta = [a * b for a, b in zip(k, beta)]
        qk = [_dot_nt(jnp.concatenate([a, b], axis=0).astype(BF16), c.astype(BF16))
              for a, b, c in zip(q, k_beta, k)]
        intra = [jnp.where(tril, a[:C] * d, 0.0).astype(BF16) for a, d in zip(qk, decay)]
        eg = [jnp.exp(g) for g in gcb]
        m = [jnp.where(strict, -(a[C:] * d), 0.0) for a, d in zip(qk, decay)]
        t = [jnp.where(ri == ci, 1.0, a) for a in m]
        n_steps = C.bit_length() - 1
        for j in range(1, n_steps):
            mb = [a.astype(BF16) for a in m]
            m = [_dot(b, b) for b in mb]
            t = [a + _dot(b.astype(BF16), a.astype(BF16)) for a, b in zip(t, m)]
        x = [_dot(a.astype(BF16), jnp.concatenate([b * c, d * f], axis=1).astype(BF16))
             for a, b, c, d, f in zip(t, v, beta, k_beta, eg)]
        gl = [g[C - 1:C, :] for g in gcb]
        xb = [a.astype(BF16) for a in x]
        kd = [(a * jnp.exp(l - g)).astype(BF16) for a, l, g in zip(k, gl, gcb)]
        kuw = [_dot_tn(a, b) for a, b in zip(kd, xb)]
        auw = [_dot(a, b) for a, b in zip(intra, xb)]
        for i, n in enumerate(ns):
            p0 = pl.multiple_of(n * (HEAD_DIM + C), HEAD_DIM + C)
            pg_s[e, pl.ds(p0, HEAD_DIM), :] = kuw[i][:, HEAD_DIM:].astype(BF16)
            pg_s[e, pl.ds(p0 + HEAD_DIM, C), :] = (q[i] * eg[i] - auw[i][:, HEAD_DIM:]).astype(BF16)
            qn_s[e, pl.ds(pl.multiple_of(n * HEAD_DIM, HEAD_DIM), HEAD_DIM), :] = kuw[i][:, :HEAD_DIM]
            on_s[e, pl.ds(r0s[i], C), :] = auw[i][:, :HEAD_DIM]
            egl_s[e, pl.ds(pl.multiple_of(n * 8, 8), 8), :] = jnp.broadcast_to(jnp.exp(gl[i]), (8, HEAD_DIM))

    for e in range(hp):
        def body(it, carry, e=e):
            prep_body(e, it)
            return carry

        lax.fori_loop(0, n_chunks // chunks_per_iter, body, 0)

    gain = gain_ref[...]

    def emit(n, o_raw):
        r0 = pl.multiple_of(n * C, C)
        for e in range(hp):
            z = z_ref[0, e, pl.ds(r0, C), :]
            o_ref[0, pl.ds(r0, C), e * LANES:(e + 1) * LANES] = (
                _rms(o_raw[e], gain) * (z * _sigmoid(z))).astype(o_ref.dtype)

    def step(n, carry):
        states, o_prev = carry
        r0 = pl.multiple_of(n * C, C)
        p0 = pl.multiple_of(n * (HEAD_DIM + C), HEAD_DIM + C)
        n0 = pl.multiple_of(n * HEAD_DIM, HEAD_DIM)
        ps = [_dot(pg_s[e, pl.ds(p0, HEAD_DIM + C), :], states[e].astype(BF16))
              for e in range(hp)]
        emit(jnp.maximum(n - 1, 0), o_prev)
        new_states = []
        for e in range(hp):
            egl = egl_s[e, pl.ds(pl.multiple_of(n * 8, 8), 1), :]
            new_states.append(states[e] * jnp.broadcast_to(egl, states[e].shape)
                              + (qn_s[e, pl.ds(n0, HEAD_DIM), :] - ps[e][:HEAD_DIM]))
        o_raw = tuple(ps[e][HEAD_DIM:] + on_s[e, pl.ds(r0, C), :] for e in range(hp))
        return tuple(new_states), o_raw

    init = (tuple(jnp.zeros((HEAD_DIM, HEAD_DIM), F32) for _ in range(hp)),
            tuple(jnp.zeros((C, HEAD_DIM), F32) for _ in range(hp)))
    _, o_last = lax.fori_loop(0, n_chunks, step, init, unroll=2)
    emit(n_chunks - 1, o_last)


def _gdn(proj, conv_w, col, row, o_gain, *, n_heads, chunks_per_iter=32):
    B, _, S, _ = proj.shape
    H = n_heads
    hp = 2 if H % 2 == 0 else 1
    n_chunks = S // GDN_CHUNK

    def slab(off):
        return pl.BlockSpec((1, hp, S, LANES), lambda b, h: (b, off // hp + h, 0, 0))

    def cw(off):
        return pl.BlockSpec((GDN_CONV, hp * LANES), lambda b, h: (0, off // hp + h))

    def per_head(rows, dtype):
        return pltpu.VMEM((hp, rows, HEAD_DIM), dtype)

    return pl.pallas_call(
        functools.partial(_gdn_kernel, n_heads=H, heads_per_step=hp,
                          chunks_per_iter=_tile(n_chunks, chunks_per_iter)),
        grid=(B, H // hp),
        in_specs=[
            slab(0), slab(H), slab(2 * H), slab(3 * H),
            cw(0), cw(H), cw(2 * H),
            pl.BlockSpec((1, S, LANES), lambda b, h: (b, 0, 0)),
            pl.BlockSpec((1, hp, n_chunks, GDN_CHUNK), lambda b, h: (b, H // hp + h, 0, 0)),
            pl.BlockSpec((1, LANES), lambda b, h: (0, 0)),
        ],
        out_specs=pl.BlockSpec((1, S, hp * LANES), lambda b, h: (b, 0, h)),
        out_shape=jax.ShapeDtypeStruct((B, S, H * HEAD_DIM), BF16),
        scratch_shapes=[
            per_head(S, F32),
            per_head(S, F32),
            per_head(n_chunks * (HEAD_DIM + GDN_CHUNK), BF16),
            per_head(n_chunks * HEAD_DIM, F32),
            per_head(S, F32),
            per_head(n_chunks * 8, F32),
            pltpu.VMEM((hp, 3, S + 8, LANES), F32),
        ],
        compiler_params=_params(2),
        name="gdn",
    )(proj, proj, proj, proj, conv_w, conv_w, conv_w, col, row, o_gain)


def _sb_kernel(q_ref, k_ref, v_ref, qg_ref, kg_ref, og_ref, o_ref, qs, ks, vs, *, group):
    T = SB_BLOCK
    G = group
    Q = G * T
    S = q_ref.shape[2]
    qs[...] = _rms(q_ref[0, 0], qg_ref[...]).astype(BF16)
    ks[...] = _rms(k_ref[0, 0], kg_ref[...]).astype(BF16)
    vs[...] = v_ref[0, 0].astype(BF16)
    scale = HEAD_DIM ** -0.5 * LOG2E
    og = og_ref[...]
    sign_bit = jnp.int32(-2 ** 31)

    ri = lax.broadcasted_iota(jnp.int32, (T, T), 0)
    ci = lax.broadcasted_iota(jnp.int32, (T, T), 1)
    past = ci < ri
    jj = lax.broadcasted_iota(jnp.int32, (2 * T, 2 * T), 0) % T
    ss = lax.broadcasted_iota(jnp.int32, (2 * T, 2 * T), 1)
    uneg = jnp.where((ss >= T) | (jj >= ss), -1.0, 0.0).astype(BF16)

    def stage_a(qb, k0, diag):
        z = _dot_nt(qb, ks[k0:k0 + Q, :]) * scale
        zs, lhs = [], []
        for d in range(G):
            zd = z[d * T if diag else 0:, d * T:(d + 1) * T]
            neg_abs = lax.bitcast_convert_type(lax.bitcast_convert_type(zd, jnp.int32) | sign_bit, F32)
            sp = jnp.maximum(zd, 0.0) + jnp.log(1.0 + jnp.exp2(neg_abs)) * LOG2E
            if diag:
                top = jnp.where(past, sp[:T], 0.0)
                sp = top if d == G - 1 else jnp.concatenate([top, sp[T:]], axis=0)
            hi = sp.astype(BF16)
            lo = (sp - hi.astype(F32)).astype(BF16)
            zs.append(zd)
            lhs.append(jnp.concatenate([hi, lo], axis=1))
        comb = _dot(lhs[0] if G == 1 else jnp.concatenate(lhs, axis=0), uneg)
        return zs, comb

    def stage_b(zs, comb, carry, k0, diag):
        offs = [0]
        for d in range(G):
            offs.append(offs[-1] + zs[d].shape[0])
        w_blocks = [None] * G
        for d in range(G - 1, -1, -1):
            lo_r = d * T if diag else 0
            cd = comb[offs[d]:offs[d + 1]]
            c_rows = carry[lo_r:]
            logw = zs[d] + c_rows + cd[:, :T]
            if diag:
                top = jnp.where(past, logw[:T], NEG_BIG)
                logw = top if d == G - 1 else jnp.concatenate([top, logw[T:]], axis=0)
            w = jnp.exp2(logw).astype(BF16)
            if lo_r:
                w = jnp.concatenate([jnp.zeros((lo_r, T), BF16), w], axis=0)
            w_blocks[d] = w
            c_new = c_rows + cd[:, T:]
            carry = c_new if lo_r == 0 else jnp.concatenate([carry[:lo_r], c_new], axis=0)
        wt = w_blocks[0] if G == 1 else jnp.concatenate(w_blocks, axis=1)
        return carry, _dot(wt, vs[k0:k0 + Q, :])

    tiles = []
    for i in range(S // Q):
        tiles.append((i, i * Q, True))
        tiles.extend((i, (i - 1 - t) * Q, False) for t in range(i))
    nxt = stage_a(qs[0:Q, :], 0, True)
    carry = acc = None
    for n, (i, k0, diag) in enumerate(tiles):
        cur = nxt
        if n + 1 < len(tiles):
            i2, k2, diag2 = tiles[n + 1]
            nxt = stage_a(qs[i2 * Q:(i2 + 1) * Q, :], k2, diag2)
        if diag:
            carry, acc = jnp.zeros((Q, T), F32), None
        carry, contrib = stage_b(cur[0], cur[1], carry, k0, diag)
        acc = contrib if acc is None else acc + contrib
        if n + 1 == len(tiles) or tiles[n + 1][2]:
            o_ref[0, i * Q:(i + 1) * Q, :] = _rms(acc, og).astype(o_ref.dtype)


def _sb(proj, q_gain, k_gain, o_gain, *, n_heads, slab0):
    B, _, S, _ = proj.shape
    H = n_heads
    group = _tile(S // SB_BLOCK, 4)

    def slab(off):
        return pl.BlockSpec((1, 1, S, LANES), lambda b, h: (b, off + h, 0, 0))

    gain = pl.BlockSpec((1, LANES), lambda b, h: (0, 0))
    return pl.pallas_call(
        functools.partial(_sb_kernel, group=group),
        grid=(B, H),
        in_specs=[slab(slab0), slab(slab0 + H), slab(slab0 + 2 * H), gain, gain, gain],
        out_specs=pl.BlockSpec((1, S, LANES), lambda b, h: (b, 0, h)),
        out_shape=jax.ShapeDtypeStruct((B, S, H * HEAD_DIM), BF16),
        scratch_shapes=[pltpu.VMEM((S, HEAD_DIM), BF16)] * 3,
        compiler_params=_params(2),
        name="sb",
    )(proj, proj, proj, q_gain, k_gain, o_gain)


def _outproj_kernel(yg_ref, ys_ref, w_ref, x_ref, g_ref, o_ref, h_ref):
    kg = yg_ref.shape[2]
    y = x_ref[0] + _dot(yg_ref[0], w_ref[:kg, :]) + _dot(ys_ref[0], w_ref[kg:, :])
    o_ref[0] = y
    h_ref[0] = _rms(y, g_ref[...]).astype(h_ref.dtype)


def _outproj(yg, ys, w, layer, x, ffn_gain, *, tm):
    B, S, D = x.shape
    Kg, Ks = yg.shape[2], ys.shape[2]
    return pl.pallas_call(
        _outproj_kernel,
        grid=(B, S // tm),
        in_specs=[
            pl.BlockSpec((1, tm, Kg), lambda b, i: (b, i, 0)),
            pl.BlockSpec((1, tm, Ks), lambda b, i: (b, i, 0)),
            pl.BlockSpec((None, Kg + Ks, D), lambda b, i: (layer, 0, 0)),
            pl.BlockSpec((1, tm, D), lambda b, i: (b, i, 0)),
            pl.BlockSpec((1, D), lambda b, i: (0, 0)),
        ],
        out_specs=[
            pl.BlockSpec((1, tm, D), lambda b, i: (b, i, 0)),
            pl.BlockSpec((1, tm, D), lambda b, i: (b, i, 0)),
        ],
        out_shape=[jax.ShapeDtypeStruct((B, S, D), F32), jax.ShapeDtypeStruct((B, S, D), BF16)],
        compiler_params=_params(2),
        name="outproj",
    )(yg, ys, w, x, ffn_gain)


def _ffn_up_kernel(h_ref, halo_ref, wg_ref, wv_ref, cg_ref, cv_ref, bg_ref, bv_ref, o_ref,
                   h_scr, ug_scr, uv_scr):
    tm = h_ref.shape[1]

    @pl.when(pl.program_id(2) == 0)
    def _():
        h_scr[:HALO, :] = jnp.where(pl.program_id(1) > 0, halo_ref[0], jnp.zeros_like(halo_ref[0]))
        h_scr[HALO:, :] = h_ref[0]

    hb = h_scr[...]
    ug_scr[...] = _dot(hb, wg_ref[...])
    uv_scr[...] = _dot(hb, wv_ref[...])

    def conv(u_scr, c_ref, b_ref):
        cw = c_ref[...]
        y = b_ref[...] + cw[FFN_CONV - 1:FFN_CONV] * u_scr[pl.ds(HALO, tm), :]
        for i in range(FFN_CONV - 1):
            y = y + cw[i:i + 1] * u_scr[pl.ds(HALO - (FFN_CONV - 1) + i, tm), :]
        return y

    gate = conv(ug_scr, cg_ref, bg_ref)
    val = conv(uv_scr, cv_ref, bv_ref)
    o_ref[0] = (gate * _sigmoid(gate) * val).astype(o_ref.dtype)


def _ffn_up(h, w_up, layer, conv_w, bias, *, tm, tn):
    B, S, D = h.shape
    F = w_up.shape[2] // 2
    nf = F // tn
    hb = tm // HALO
    return pl.pallas_call(
        _ffn_up_kernel,
        grid=(B, S // tm, nf),
        in_specs=[
            pl.BlockSpec((1, tm, D), lambda b, i, j: (b, i, 0)),
            pl.BlockSpec((1, HALO, D), lambda b, i, j: (b, jnp.maximum(i * hb - 1, 0), 0)),
            pl.BlockSpec((None, D, tn), lambda b, i, j: (layer, 0, j)),
            pl.BlockSpec((None, D, tn), lambda b, i, j: (layer, 0, nf + j)),
            pl.BlockSpec((FFN_CONV, tn), lambda b, i, j: (0, j)),
            pl.BlockSpec((FFN_CONV, tn), lambda b, i, j: (0, nf + j)),
            pl.BlockSpec((1, tn), lambda b, i, j: (0, j)),
            pl.BlockSpec((1, tn), lambda b, i, j: (0, nf + j)),
        ],
        out_specs=pl.BlockSpec((1, tm, tn), lambda b, i, j: (b, i, j)),
        out_shape=jax.ShapeDtypeStruct((B, S, F), BF16),
        scratch_shapes=[
            pltpu.VMEM((tm + HALO, D), BF16),
            pltpu.VMEM((tm + HALO, tn), F32),
            pltpu.VMEM((tm + HALO, tn), F32),
        ],
        compiler_params=_params(3),
        name="ffn_up",
    )(h, h, w_up, w_up, conv_w, conv_w, bias, bias)


def _ffn_down_kernel(a_ref, w_ref, x_ref, o_ref):
    o_ref[0] = x_ref[0] + _dot(a_ref[0], w_ref[...])


def _ffn_down(act, w, layer, x, *, tm, tn):
    B, S, D = x.shape
    F = act.shape[2]
    return pl.pallas_call(
        _ffn_down_kernel,
        grid=(B, S // tm, D // tn),
        in_specs=[
            pl.BlockSpec((1, tm, F), lambda b, i, j: (b, i, 0)),
            pl.BlockSpec((None, F, tn), lambda b, i, j: (layer, 0, j)),
            pl.BlockSpec((1, tm, tn), lambda b, i, j: (b, i, j)),
        ],
        out_specs=pl.BlockSpec((1, tm, tn), lambda b, i, j: (b, i, j)),
        out_shape=jax.ShapeDtypeStruct((B, S, D), F32),
        compiler_params=_params(3),
        name="ffn_down",
    )(act, w, x)


def _tile(n, pref):
    t = min(n, pref)
    while n % t:
        t //= 2
    return t


def kernel(x, attn_norm, w_in, gdn_conv, gdn_a_log, gdn_dt_bias, gdn_o_norm, sb_q_norm, sb_k_norm, sb_o_norm,
           w_out, ffn_norm, w_up, ffn_conv, ffn_conv_bias, w_down):
    B, S, D = x.shape
    depth = w_in.shape[0]
    H = gdn_a_log.shape[1]
    W = H * HEAD_DIM
    assert w_in.shape[2] == 7 * W + 2 * H and S % LANES == 0 and 2 * H <= LANES
    F = w_down.shape[1]
    tm = _tile(S, 1024)
    tn_in = _tile(W, 1024)
    tn_d = _tile(D, 512)
    tm_o = _tile(S, 512)
    tn_f = _tile(F, 512)

    def lane_row(v, off):
        return jnp.zeros((1, LANES), F32).at[0, off:off + v.shape[0]].set(v)

    w_in_b = w_in.astype(BF16)
    w_sb = w_in_b[..., 4 * W + 2 * H:]
    w_ba = jnp.pad(w_in[..., 4 * W:4 * W + 2 * H], ((0, 0), (0, 0), (0, LANES - 2 * H))).astype(BF16)
    w_out_b, w_up_b, w_down_b = w_out.astype(BF16), w_up.astype(BF16), w_down.astype(BF16)
    for l in range(depth):
        proj, ba = _inproj(x, attn_norm[l][None], w_in_b, 4 * W // tn_in, w_sb, w_ba, l, tm=tm, tn=tn_in)
        col, row = _gdn_prep(ba, lane_row(gdn_a_log[l], H), lane_row(gdn_dt_bias[l], H), n_heads=H)
        row = row.reshape(B, LANES, S // GDN_CHUNK, GDN_CHUNK)
        y_gdn = _gdn(proj, gdn_conv[l], col, row, gdn_o_norm[l][None], n_heads=H)
        y_sb = _sb(proj, sb_q_norm[l][None], sb_k_norm[l][None], sb_o_norm[l][None], n_heads=H, slab0=4 * H)
        x, h_ffn = _outproj(y_gdn, y_sb, w_out_b, l, x, ffn_norm[l][None], tm=tm_o)
        act = _ffn_up(h_ffn, w_up_b, l, ffn_conv[l], ffn_conv_bias[l][None], tm=tm, tn=tn_f)
        x = _ffn_down(act, w_down_b, l, x, tm=tm, tn=tn_d)
    return x
```

```python
import functools

import jax
import jax.numpy as jnp
from jax import lax
from jax.experimental import pallas as pl
from jax.experimental.pallas import tpu as pltpu

F32 = jnp.float32
BF16 = jnp.bfloat16
RMS_EPS = 1e-6
L2_EPS = 1e-6
HEAD_DIM = 128
LANES = 128
GDN_CHUNK = 64
GDN_CONV = 4
FFN_CONV = 3
SB_BLOCK = 128
HALO = 16
NEG_BIG = -1e30
LOG2E = 1.4426950408889634
VMEM_LIMIT = 56 * 1024 * 1024


def _dot(a, b):
    return jnp.dot(a, b, preferred_element_type=F32)


def _dot_nt(a, b):
    return lax.dot_general(a, b, (((1,), (1,)), ((), ())), preferred_element_type=F32)


def _dot_tn(a, b):
    return lax.dot_general(a, b, (((0,), (0,)), ((), ())), preferred_element_type=F32)


def _sigmoid(x):
    return 1.0 / (1.0 + jnp.exp(-x))


def _softplus(x):
    return jnp.maximum(x, 0.0) + jnp.log1p(jnp.exp(-jnp.abs(x)))


def _rms(x, gain):
    return (x * lax.rsqrt(jnp.mean(x * x, axis=-1, keepdims=True) + RMS_EPS)) * gain


def _params(n_axes):
    return pltpu.CompilerParams(dimension_semantics=("arbitrary",) * n_axes, vmem_limit_bytes=VMEM_LIMIT)


def _inproj_kernel(x_ref, g_ref, wa_ref, wb_ref, wba_ref, o_ref, ba_ref, h_scr, *, n_slabs, n_a):
    j = pl.program_id(2)

    @pl.when(j == 0)
    def _():
        hb = _rms(x_ref[0], g_ref[...]).astype(BF16)
        h_scr[...] = hb
        ba_ref[0] = _dot(hb, wba_ref[...])

    def emit(w_ref):
        res = _dot(h_scr[...], w_ref[...])
        for c in range(n_slabs):
            o_ref[0, c] = res[:, c * LANES:(c + 1) * LANES]

    pl.when(j < n_a)(lambda: emit(wa_ref))
    pl.when(j >= n_a)(lambda: emit(wb_ref))


def _inproj(x, gain, w_a, n_a, w_b, w_ba, layer, *, tm, tn):
    B, S, D = x.shape
    n_b = w_b.shape[2] // tn
    n_slabs = tn // LANES
    return pl.pallas_call(
        functools.partial(_inproj_kernel, n_slabs=n_slabs, n_a=n_a),
        grid=(B, S // tm, n_a + n_b),
        in_specs=[
            pl.BlockSpec((1, tm, D), lambda b, i, j: (b, i, 0)),
            pl.BlockSpec((1, D), lambda b, i, j: (0, 0)),
            pl.BlockSpec((None, D, tn), lambda b, i, j: (layer, 0, jnp.minimum(j, n_a - 1))),
            pl.BlockSpec((None, D, tn), lambda b, i, j: (layer, 0, jnp.where(j < n_a, n_b - 1, j - n_a))),
            pl.BlockSpec((None, D, LANES), lambda b, i, j: (layer, 0, 0)),
        ],
        out_specs=[
            pl.BlockSpec((1, n_slabs, tm, LANES), lambda b, i, j: (b, j, i, 0)),
            pl.BlockSpec((1, tm, LANES), lambda b, i, j: (b, i, 0)),
        ],
        out_shape=[
            jax.ShapeDtypeStruct((B, (n_a + n_b) * n_slabs, S, LANES), F32),
            jax.ShapeDtypeStruct((B, S, LANES), F32),
        ],
        scratch_shapes=[pltpu.VMEM((tm, D), BF16)],
        compiler_params=_params(3),
        name="inproj",
    )(x, gain, w_a, w_b, w_ba)


def _gdn_prep_kernel(ba_ref, alog_ref, dtb_ref, col_ref, row_ref, *, n_heads):
    ba = ba_ref[0]
    S = ba.shape[0]
    g = -jnp.exp(alog_ref[...]) * _softplus(ba + dtb_ref[...])
    r = lax.broadcasted_iota(jnp.int32, (LANES, LANES), 0)
    c = lax.broadcasted_iota(jnp.int32, (LANES, LANES), 1)
    tril = jnp.where((r // GDN_CHUNK == c // GDN_CHUNK) & (c <= r), 1.0, 0.0).astype(F32)
    parts = [
        jnp.dot(tril, g[i * LANES:(i + 1) * LANES], precision=lax.Precision.HIGHEST, preferred_element_type=F32)
        for i in range(S // LANES)
    ]
    gc = jnp.concatenate(parts, axis=0)
    lane = lax.broadcasted_iota(jnp.int32, ba.shape, 1)
    col = jnp.where(lane < n_heads, _sigmoid(ba), gc)
    col_ref[0] = col
    row_ref[0] = col.T


def _gdn_prep(ba, alog_l, dtb_l, *, n_heads):
    B, S, _ = ba.shape
    return pl.pallas_call(
        functools.partial(_gdn_prep_kernel, n_heads=n_heads),
        grid=(B,),
        in_specs=[
            pl.BlockSpec((1, S, LANES), lambda b: (b, 0, 0)),
            pl.BlockSpec((1, LANES), lambda b: (0, 0)),
            pl.BlockSpec((1, LANES), lambda b: (0, 0)),
        ],
        out_specs=[
            pl.BlockSpec((1, S, LANES), lambda b: (b, 0, 0)),
            pl.BlockSpec((1, LANES, S), lambda b: (b, 0, 0)),
        ],
        out_shape=[
            jax.ShapeDtypeStruct((B, S, LANES), F32),
            jax.ShapeDtypeStruct((B, LANES, S), F32),
        ],
        compiler_params=_params(1),
        name="gdn_prep",
    )(ba, alog_l, dtb_l)


def _gdn_kernel(q_ref, k_ref, v_ref, z_ref, cwq_ref, cwk_ref, cwv_ref, col_ref, row_ref, gain_ref, o_ref,
                beta_s, gc_s, pg_s, qn_s, on_s, egl_s, xpad_s, *, n_heads, heads_per_step, chunks_per_iter):
    C = GDN_CHUNK
    PAD = 8
    S = col_ref.shape[1]
    n_chunks = S // C
    hp = heads_per_step
    h0 = pl.program_id(1) * hp

    colv = col_ref[0]
    lane = lax.broadcasted_iota(jnp.int32, colv.shape, 1)
    for e in range(hp):
        beta_s[e] = jnp.broadcast_to(
            jnp.sum(jnp.where(lane == h0 + e, colv, 0.0), axis=1, keepdims=True), colv.shape)
        gc_s[e] = jnp.broadcast_to(
            jnp.sum(jnp.where(lane == h0 + e + n_heads, colv, 0.0), axis=1, keepdims=True), colv.shape)

    ri = lax.broadcasted_iota(jnp.int32, (C, C), 0)
    ci = lax.broadcasted_iota(jnp.int32, (C, C), 1)
    tril = ci <= ri
    strict = ci < ri

    for e in range(hp):
        for t, x_ref in enumerate((q_ref, k_ref, v_ref)):
            xpad_s[e, t, 0:PAD, :] = jnp.zeros((PAD, LANES), F32)
            xpad_s[e, t, PAD:, :] = x_ref[0, e]

    def conv_silu(t, cw_ref, e, r0):
        cw = cw_ref[:, e * LANES:(e + 1) * LANES]
        y = cw[GDN_CONV - 1:GDN_CONV] * xpad_s[e, t, pl.ds(r0 + PAD, C), :]
        for i in range(GDN_CONV - 1):
            y = y + cw[i:i + 1] * xpad_s[e, t, pl.ds(r0 + PAD - (GDN_CONV - 1) + i, C), :]
        half = 0.5 * y
        return half + half * jnp.tanh(half)

    def l2n(x):
        return x * lax.rsqrt(jnp.sum(x * x, axis=-1, keepdims=True) + L2_EPS)

    def prep_body(e, it):
        ns = [it * chunks_per_iter + s for s in range(chunks_per_iter)]
        r0s = [pl.multiple_of(n * C, C) for n in ns]
        q = [l2n(conv_silu(0, cwq_ref, e, r0)) * (HEAD_DIM ** -0.5) for r0 in r0s]
        k = [l2n(conv_silu(1, cwk_ref, e, r0)) for r0 in r0s]
        v = [conv_silu(2, cwv_ref, e, r0) for r0 in r0s]
        beta = [beta_s[e, pl.ds(r0, C), :] for r0 in r0s]
        gcb = [gc_s[e, pl.ds(r0, C), :] for r0 in r0s]
        decay = [jnp.exp(jnp.where(tril, g[:, :C] - row_ref[0, e, pl.ds(n, 1), :], NEG_BIG))
                 for n, g in zip(ns, gcb)]
        k_beta = [a * b for a, b in zip(k, beta)]
        qk = [_dot_nt(jnp.concatenate([a, b], axis=0).astype(BF16), c.astype(BF16))
              for a, b, c in zip(q, k_beta, k)]
        intra = [jnp.where(tril, a[:C] * d, 0.0).astype(BF16) for a, d in zip(qk, decay)]
        eg = [jnp.exp(g) for g in gcb]
        m = [jnp.where(strict, -(a[C:] * d), 0.0) for a, d in zip(qk, decay)]
        t = [jnp.where(ri == ci, 1.0, a) for a in m]
        n_steps = C.bit_length() - 1
        for j in range(1, n_steps):
            mb = [a.astype(BF16) for a in m]
            m = [_dot(b, b) for b in mb]
            t = [a + _dot(b.astype(BF16), a.astype(BF16)) for a, b in zip(t, m)]
        x = [_dot(a.astype(BF16), jnp.concatenate([b * c, d * f], axis=1).astype(BF16))
             for a, b, c, d, f in zip(t, v, beta, k_beta, eg)]
        gl = [g[C - 1:C, :] for g in gcb]
        xb = [a.astype(BF16) for a in x]
        kd = [(a * jnp.exp(l - g)).astype(BF16) for a, l, g in zip(k, gl, gcb)]
        kuw = [_dot_tn(a, b) for a, b in zip(kd, xb)]
        auw = [_dot(a, b) for a, b in zip(intra, xb)]
        for i, n in enumerate(ns):
            p0 = pl.multiple_of(n * (HEAD_DIM + C), HEAD_DIM + C)
            pg_s[e, pl.ds(p0, HEAD_DIM), :] = kuw[i][:, HEAD_DIM:].astype(BF16)
            pg_s[e, pl.ds(p0 + HEAD_DIM, C), :] = (q[i] * eg[i] - auw[i][:, HEAD_DIM:]).astype(BF16)
            qn_s[e, pl.ds(pl.multiple_of(n * HEAD_DIM, HEAD_DIM), HEAD_DIM), :] = kuw[i][:, :HEAD_DIM]
            on_s[e, pl.ds(r0s[i], C), :] = auw[i][:, :HEAD_DIM]
            egl_s[e, pl.ds(pl.multiple_of(n * 8, 8), 8), :] = jnp.broadcast_to(jnp.exp(gl[i]), (8, HEAD_DIM))

    for e in range(hp):
        def body(it, carry, e=e):
            prep_body(e, it)
            return carry

        lax.fori_loop(0, n_chunks // chunks_per_iter, body, 0)

    gain = gain_ref[...]

    def emit(n, o_raw):
        r0 = pl.multiple_of(n * C, C)
        for e in range(hp):
            z = z_ref[0, e, pl.ds(r0, C), :]
            o_ref[0, pl.ds(r0, C), e * LANES:(e + 1) * LANES] = (
                _rms(o_raw[e], gain) * (z * _sigmoid(z))).astype(o_ref.dtype)

    def step(n, carry):
        states, o_prev = carry
        r0 = pl.multiple_of(n * C, C)
        p0 = pl.multiple_of(n * (HEAD_DIM + C), HEAD_DIM + C)
        n0 = pl.multiple_of(n * HEAD_DIM, HEAD_DIM)
        ps = [_dot(pg_s[e, pl.ds(p0, HEAD_DIM + C), :], states[e].astype(BF16))
              for e in range(hp)]
        emit(jnp.maximum(n - 1, 0), o_prev)
        new_states = []
        for e in range(hp):
            egl = egl_s[e, pl.ds(pl.multiple_of(n * 8, 8), 1), :]
            new_states.append(states[e] * jnp.broadcast_to(egl, states[e].shape)
                              + (qn_s[e, pl.ds(n0, HEAD_DIM), :] - ps[e][:HEAD_DIM]))
        o_raw = tuple(ps[e][HEAD_DIM:] + on_s[e, pl.ds(r0, C), :] for e in range(hp))
        return tuple(new_states), o_raw

    init = (tuple(jnp.zeros((HEAD_DIM, HEAD_DIM), F32) for _ in range(hp)),
            tuple(jnp.zeros((C, HEAD_DIM), F32) for _ in range(hp)))
    _, o_last = lax.fori_loop(0, n_chunks, step, init, unroll=4)
    emit(n_chunks - 1, o_last)


def _gdn(proj, conv_w, col, row, o_gain, *, n_heads, chunks_per_iter=32):
    B, _, S, _ = proj.shape
    H = n_heads
    hp = 2 if H % 2 == 0 else 1
    n_chunks = S // GDN_CHUNK

    def slab(off):
        return pl.BlockSpec((1, hp, S, LANES), lambda b, h: (b, off // hp + h, 0, 0))

    def cw(off):
        return pl.BlockSpec((GDN_CONV, hp * LANES), lambda b, h: (0, off // hp + h))

    def per_head(rows, dtype):
        return pltpu.VMEM((hp, rows, HEAD_DIM), dtype)

    return pl.pallas_call(
        functools.partial(_gdn_kernel, n_heads=H, heads_per_step=hp,
                          chunks_per_iter=_tile(n_chunks, chunks_per_iter)),
        grid=(B, H // hp),
        in_specs=[
            slab(0), slab(H), slab(2 * H), slab(3 * H),
            cw(0), cw(H), cw(2 * H),
            pl.BlockSpec((1, S, LANES), lambda b, h: (b, 0, 0)),
            pl.BlockSpec((1, hp, n_chunks, GDN_CHUNK), lambda b, h: (b, H // hp + h, 0, 0)),
            pl.BlockSpec((1, LANES), lambda b, h: (0, 0)),
        ],
        out_specs=pl.BlockSpec((1, S, hp * LANES), lambda b, h: (b, 0, h)),
        out_shape=jax.ShapeDtypeStruct((B, S, H * HEAD_DIM), BF16),
        scratch_shapes=[
            per_head(S, F32),
            per_head(S, F32),
            per_head(n_chunks * (HEAD_DIM + GDN_CHUNK), BF16),
            per_head(n_chunks * HEAD_DIM, F32),
            per_head(S, F32),
            per_head(n_chunks * 8, F32),
            pltpu.VMEM((hp, 3, S + 8, LANES), F32),
        ],
        compiler_params=_params(2),
        name="gdn",
    )(proj, proj, proj, proj, conv_w, conv_w, conv_w, col, row, o_gain)


def _sb_kernel(q_ref, k_ref, v_ref, qg_ref, kg_ref, og_ref, o_ref, qs, ks, vs, *, group):
    T = SB_BLOCK
    G = group
    Q = G * T
    S = q_ref.shape[2]
    qs[...] = _rms(q_ref[0, 0], qg_ref[...]).astype(BF16)
    ks[...] = _rms(k_ref[0, 0], kg_ref[...]).astype(BF16)
    vs[...] = v_ref[0, 0].astype(BF16)
    scale = HEAD_DIM ** -0.5 * LOG2E
    og = og_ref[...]
    sign_bit = jnp.int32(-2 ** 31)

    ri = lax.broadcasted_iota(jnp.int32, (T, T), 0)
    ci = lax.broadcasted_iota(jnp.int32, (T, T), 1)
    past = ci < ri
    jj = lax.broadcasted_iota(jnp.int32, (2 * T, 2 * T), 0) % T
    ss = lax.broadcasted_iota(jnp.int32, (2 * T, 2 * T), 1)
    uneg = jnp.where((ss >= T) | (jj >= ss), -1.0, 0.0).astype(BF16)

    def stage_a(qb, k0, diag):
        z = _dot_nt(qb, ks[k0:k0 + Q, :]) * scale
        zs, lhs = [], []
        for d in range(G):
            zd = z[d * T if diag else 0:, d * T:(d + 1) * T]
            neg_abs = lax.bitcast_convert_type(lax.bitcast_convert_type(zd, jnp.int32) | sign_bit, F32)
            sp = jnp.maximum(zd, 0.0) + jnp.log(1.0 + jnp.exp2(neg_abs)) * LOG2E
            if diag:
                top = jnp.where(past, sp[:T], 0.0)
                sp = top if d == G - 1 else jnp.concatenate([top, sp[T:]], axis=0)
            hi = sp.astype(BF16)
            lo = (sp - hi.astype(F32)).astype(BF16)
            zs.append(zd)
            lhs.append(jnp.concatenate([hi, lo], axis=1))
        comb = _dot(lhs[0] if G == 1 else jnp.concatenate(lhs, axis=0), uneg)
        return zs, comb

    def stage_b(zs, comb, carry, k0, diag):
        offs = [0]
        for d in range(G):
            offs.append(offs[-1] + zs[d].shape[0])
        w_blocks = [None] * G
        for d in range(G - 1, -1, -1):
            lo_r = d * T if diag else 0
            cd = comb[offs[d]:offs[d + 1]]
            c_rows = carry[lo_r:]
            logw = zs[d] + c_rows + cd[:, :T]
            if diag:
                top = jnp.where(past, logw[:T], NEG_BIG)
                logw = top if d == G - 1 else jnp.concatenate([top, logw[T:]], axis=0)
            w = jnp.exp2(logw).astype(BF16)
            if lo_r:
                w = jnp.concatenate([jnp.zeros((lo_r, T), BF16), w], axis=0)
            w_blocks[d] = w
            c_new = c_rows + cd[:, T:]
            carry = c_new if lo_r == 0 else jnp.concatenate([carry[:lo_r], c_new], axis=0)
        wt = w_blocks[0] if G == 1 else jnp.concatenate(w_blocks, axis=1)
        return carry, _dot(wt, vs[k0:k0 + Q, :])

    tiles = []
    for i in range(S // Q):
        tiles.append((i, i * Q, True))
        tiles.extend((i, (i - 1 - t) * Q, False) for t in range(i))
    nxt = stage_a(qs[0:Q, :], 0, True)
    carry = acc = None
    for n, (i, k0, diag) in enumerate(tiles):
        cur = nxt
        if n + 1 < len(tiles):
            i2, k2, diag2 = tiles[n + 1]
            nxt = stage_a(qs[i2 * Q:(i2 + 1) * Q, :], k2, diag2)
        if diag:
            carry, acc = jnp.zeros((Q, T), F32), None
        carry, contrib = stage_b(cur[0], cur[1], carry, k0, diag)
        acc = contrib if acc is None else acc + contrib
        if n + 1 == len(tiles) or tiles[n + 1][2]:
            o_ref[0, i * Q:(i + 1) * Q, :] = _rms(acc, og).astype(o_ref.dtype)


def _sb(proj, q_gain, k_gain, o_gain, *, n_heads, slab0):
    B, _, S, _ = proj.shape
    H = n_heads
    group = _tile(S // SB_BLOCK, 4)

    def slab(off):
        return pl.BlockSpec((1, 1, S, LANES), lambda b, h: (b, off + h, 0, 0))

    gain = pl.BlockSpec((1, LANES), lambda b, h: (0, 0))
    return pl.pallas_call(
        functools.partial(_sb_kernel, group=group),
        grid=(B, H),
        in_specs=[slab(slab0), slab(slab0 + H), slab(slab0 + 2 * H), gain, gain, gain],
        out_specs=pl.BlockSpec((1, S, LANES), lambda b, h: (b, 0, h)),
        out_shape=jax.ShapeDtypeStruct((B, S, H * HEAD_DIM), BF16),
        scratch_shapes=[pltpu.VMEM((S, HEAD_DIM), BF16)] * 3,
        compiler_params=_params(2),
        name="sb",
    )(proj, proj, proj, q_gain, k_gain, o_gain)


def _outproj_kernel(yg_ref, ys_ref, w_ref, x_ref, g_ref, o_ref, h_ref):
    kg = yg_ref.shape[2]
    y = x_ref[0] + _dot(yg_ref[0], w_ref[:kg, :]) + _dot(ys_ref[0], w_ref[kg:, :])
    o_ref[0] = y
    h_ref[0] = _rms(y, g_ref[...]).astype(h_ref.dtype)


def _outproj(yg, ys, w, layer, x, ffn_gain, *, tm):
    B, S, D = x.shape
    Kg, Ks = yg.shape[2], ys.shape[2]
    return pl.pallas_call(
        _outproj_kernel,
        grid=(B, S // tm),
        in_specs=[
            pl.BlockSpec((1, tm, Kg), lambda b, i: (b, i, 0)),
            pl.BlockSpec((1, tm, Ks), lambda b, i: (b, i, 0)),
            pl.BlockSpec((None, Kg + Ks, D), lambda b, i: (layer, 0, 0)),
            pl.BlockSpec((1, tm, D), lambda b, i: (b, i, 0)),
            pl.BlockSpec((1, D), lambda b, i: (0, 0)),
        ],
        out_specs=[
            pl.BlockSpec((1, tm, D), lambda b, i: (b, i, 0)),
            pl.BlockSpec((1, tm, D), lambda b, i: (b, i, 0)),
        ],
        out_shape=[jax.ShapeDtypeStruct((B, S, D), F32), jax.ShapeDtypeStruct((B, S, D), BF16)],
        compiler_params=_params(2),
        name="outproj",
    )(yg, ys, w, x, ffn_gain)


def _ffn_up_kernel(h_ref, halo_ref, wg_ref, wv_ref, cg_ref, cv_ref, bg_ref, bv_ref, o_ref,
                   h_scr, ug_scr, uv_scr):
    tm = h_ref.shape[1]

    @pl.when(pl.program_id(2) == 0)
    def _():
        h_scr[:HALO, :] = jnp.where(pl.program_id(1) > 0, halo_ref[0], jnp.zeros_like(halo_ref[0]))
        h_scr[HALO:, :] = h_ref[0]

    hb = h_scr[...]
    ug_scr[...] = _dot(hb, wg_ref[...])
    uv_scr[...] = _dot(hb, wv_ref[...])

    def conv(u_scr, c_ref, b_ref):
        cw = c_ref[...]
        y = b_ref[...] + cw[FFN_CONV - 1:FFN_CONV] * u_scr[pl.ds(HALO, tm), :]
        for i in range(FFN_CONV - 1):
            y = y + cw[i:i + 1] * u_scr[pl.ds(HALO - (FFN_CONV - 1) + i, tm), :]
        return y

    gate = conv(ug_scr, cg_ref, bg_ref)
    val = conv(uv_scr, cv_ref, bv_ref)
    o_ref[0] = (gate * _sigmoid(gate) * val).astype(o_ref.dtype)


def _ffn_up(h, w_up, layer, conv_w, bias, *, tm, tn):
    B, S, D = h.shape
    F = w_up.shape[2] // 2
    nf = F // tn
    hb = tm // HALO
    return pl.pallas_call(
        _ffn_up_kernel,
        grid=(B, S // tm, nf),
        in_specs=[
            pl.BlockSpec((1, tm, D), lambda b, i, j: (b, i, 0)),
            pl.BlockSpec((1, HALO, D), lambda b, i, j: (b, jnp.maximum(i * hb - 1, 0), 0)),
            pl.BlockSpec((None, D, tn), lambda b, i, j: (layer, 0, j)),
            pl.BlockSpec((None, D, tn), lambda b, i, j: (layer, 0, nf + j)),
            pl.BlockSpec((FFN_CONV, tn), lambda b, i, j: (0, j)),
            pl.BlockSpec((FFN_CONV, tn), lambda b, i, j: (0, nf + j)),
            pl.BlockSpec((1, tn), lambda b, i, j: (0, j)),
            pl.BlockSpec((1, tn), lambda b, i, j: (0, nf + j)),
        ],
        out_specs=pl.BlockSpec((1, tm, tn), lambda b, i, j: (b, i, j)),
        out_shape=jax.ShapeDtypeStruct((B, S, F), BF16),
        scratch_shapes=[
            pltpu.VMEM((tm + HALO, D), BF16),
            pltpu.VMEM((tm + HALO, tn), F32),
            pltpu.VMEM((tm + HALO, tn), F32),
        ],
        compiler_params=_params(3),
        name="ffn_up",
    )(h, h, w_up, w_up, conv_w, conv_w, bias, bias)


def _ffn_down_kernel(a_ref, w_ref, x_ref, o_ref):
    o_ref[0] = x_ref[0] + _dot(a_ref[0], w_ref[...])


def _ffn_down(act, w, layer, x, *, tm, tn):
    B, S, D = x.shape
    F = act.shape[2]
    return pl.pallas_call(
        _ffn_down_kernel,
        grid=(B, S // tm, D // tn),
        in_specs=[
            pl.BlockSpec((1, tm, F), lambda b, i, j: (b, i, 0)),
            pl.BlockSpec((None, F, tn), lambda b, i, j: (layer, 0, j)),
            pl.BlockSpec((1, tm, tn), lambda b, i, j: (b, i, j)),
        ],
        out_specs=pl.BlockSpec((1, tm, tn), lambda b, i, j: (b, i, j)),
        out_shape=jax.ShapeDtypeStruct((B, S, D), F32),
        compiler_params=_params(3),
        name="ffn_down",
    )(act, w, x)


def _tile(n, pref):
    t = min(n, pref)
    while n % t:
        t //= 2
    return t


def kernel(x, attn_norm, w_in, gdn_conv, gdn_a_log, gdn_dt_bias, gdn_o_norm, sb_q_norm, sb_k_norm, sb_o_norm,
           w_out, ffn_norm, w_up, ffn_conv, ffn_conv_bias, w_down):
    B, S, D = x.shape
    depth = w_in.shape[0]
    H = gdn_a_log.shape[1]
    W = H * HEAD_DIM
    assert w_in.shape[2] == 7 * W + 2 * H and S % LANES == 0 and 2 * H <= LANES
    F = w_down.shape[1]
    tm = _tile(S, 1024)
    tn_in = _tile(W, 1024)
    tn_d = _tile(D, 512)
    tm_o = _tile(S, 512)
    tn_f = _tile(F, 512)

    def lane_row(v, off):
        return jnp.zeros((1, LANES), F32).at[0, off:off + v.shape[0]].set(v)

    w_in_b = w_in.astype(BF16)
    w_sb = w_in_b[..., 4 * W + 2 * H:]
    w_ba = jnp.pad(w_in[..., 4 * W:4 * W + 2 * H], ((0, 0), (0, 0), (0, LANES - 2 * H))).astype(BF16)
    w_out_b, w_up_b, w_down_b = w_out.astype(BF16), w_up.astype(BF16), w_down.astype(BF16)
    for l in range(depth):
        proj, ba = _inproj(x, attn_norm[l][None], w_in_b, 4 * W // tn_in, w_sb, w_ba, l, tm=tm, tn=tn_in)
        col, row = _gdn_prep(ba, lane_row(gdn_a_log[l], H), lane_row(gdn_dt_bias[l], H), n_heads=H)
        row = row.reshape(B, LANES, S // GDN_CHUNK, GDN_CHUNK)
        y_gdn = _gdn(proj, gdn_conv[l], col, row, gdn_o_norm[l][None], n_heads=H)
        y_sb = _sb(proj, sb_q_norm[l][None], sb_k_norm[l][None], sb_o_norm[l][None], n_heads=H, slab0=4 * H)
        x, h_ffn = _outproj(y_gdn, y_sb, w_out_b, l, x, ffn_norm[l][None], tm=tm_o)
        act = _ffn_up(h_ffn, w_up_b, l, ffn_conv[l], ffn_conv_bias[l][None], tm=tm, tn=tn_f)
        x = _ffn_down(act, w_down_b, l, x, tm=tm, tn=tn_d)
    return x
```
